```python
import math
import jax, jax.numpy as jnp
from jax import lax
import numpy as np

D_MODEL = 2048
BATCH = 32
SEQ = 256
DEPTH = 4
DEC_BATCH = 4
DEC_SEQ = 4096
PAST_LEN = 512

GRID_W = 64
N_MIXERS = 2
N_DIFF = (DEPTH + 1) // 2
N_GQA = DEPTH // 2
DIFF_DH = 128
DIFF_HEADS = D_MODEL // (2 * DIFF_DH)
GQA_DH = 128
GQA_HEADS = D_MODEL // GQA_DH
GQA_KV = GQA_HEADS // 2
GQA_GROUP = GQA_HEADS // GQA_KV
ROPE_DIM = 128
ROPE_THETA = 10000.0
Q_BLOCK = 128
N_EXPERTS = 32
TOP_K = 4
D_FF = D_MODEL
SWIGLU_ALPHA = 1.702
SWIGLU_LIMIT = 7.0
EXPERT_BLOCK = 128
EPS = 1e-6
N_MOD = 6

kernel_name = 'hybrid_diff_gqa_moe_diffusion_step'


def rmsnorm(x, g):
    xf = x.astype(jnp.float32)
    y = xf * lax.rsqrt(jnp.mean(xf * xf, axis=-1, keepdims=True) + EPS)
    return (y * g.astype(jnp.float32)).astype(x.dtype)


def lambda_init(i):
    return 0.8 - 0.6 * math.exp(-0.3 * i)


def grid_rope(T, dtype):
    rows = T // GRID_W
    row = jnp.repeat(jnp.arange(rows), GRID_W).astype(jnp.float32)
    col = jnp.tile(jnp.arange(GRID_W), rows).astype(jnp.float32)
    nf = ROPE_DIM // 4
    inv = ROPE_THETA ** (-jnp.arange(nf, dtype=jnp.float32) / nf)
    ar = row[:, None] * inv
    ac = col[:, None] * inv
    ang = jnp.concatenate([ar, ar, ac, ac], axis=-1)
    return jnp.cos(ang).astype(dtype), jnp.sin(ang).astype(dtype)


def rot_axial(x):
    x1, x2, x3, x4 = jnp.split(x, 4, axis=-1)
    return jnp.concatenate([-x2, x1, -x4, x3], axis=-1)


def apply_rope(x, rope):
    cos, sin = rope
    shape = (1, cos.shape[0]) + (1,) * (x.ndim - 3) + (cos.shape[-1],)
    return x * cos.reshape(shape) + rot_axial(x) * sin.reshape(shape)


def sweep_query_blocks(fn, q):
    B, T = q.shape[:2]
    nb = T // Q_BLOCK
    qb = jnp.moveaxis(q.reshape((B, nb, Q_BLOCK) + q.shape[2:]), 1, 0)
    ob = lax.map(fn, qb)
    return jnp.moveaxis(ob, 0, 1).reshape((B, T) + ob.shape[3:])


def diff_attention(h, w_qkv, w_o, lam_p, subln_g, lam_init, rope, ctx_k, ctx_v):
    B, T, _ = h.shape
    w = DIFF_HEADS * 2 * DIFF_DH
    qkv = h @ w_qkv
    q = qkv[..., :w].reshape(B, T, DIFF_HEADS, 2, DIFF_DH)
    k = qkv[..., w:2 * w].reshape(B, T, DIFF_HEADS, 2, DIFF_DH)
    v = qkv[..., 2 * w:].reshape(B, T, DIFF_HEADS, 2 * DIFF_DH)
    new_k = k.reshape(B, T, DIFF_HEADS, 2 * DIFF_DH)
    new_v = v
    if rope is not None:
        q = apply_rope(q, rope)
        k = apply_rope(k, rope)
    if ctx_k is not None:
        L = ctx_k.shape[1]
        k = jnp.concatenate([ctx_k.reshape(B, L, DIFF_HEADS, 2, DIFF_DH), k], axis=1)
        v = jnp.concatenate([ctx_v, v], axis=1)
    lp = lam_p.astype(jnp.float32)
    lam = jnp.exp(jnp.sum(lp[0] * lp[1])) - jnp.exp(jnp.sum(lp[2] * lp[3])) + lam_init
    scale = DIFF_DH ** -0.5

    def block(qb):
        s = jnp.einsum('bqhmd,bshmd->bhmqs', qb, k, preferred_element_type=jnp.float32) * scale
        p = jax.nn.softmax(s, axis=-1)
        a = (p[:, :, 0] - lam * p[:, :, 1]).astype(v.dtype)
        return jnp.einsum('bhqs,bshe->bqhe', a, v)

    o = sweep_query_blocks(block, q)
    o = rmsnorm(o, subln_g) * (1.0 - lam_init)
    return o.reshape(B, T, w) @ w_o, new_k, new_v


def gqa_attention(h, w_qkv, w_o, q_g, k_g, rope, ctx_k, ctx_v):
    B, T, _ = h.shape
    wq = GQA_HEADS * GQA_DH
    wk = GQA_KV * GQA_DH
    qkv = h @ w_qkv
    q = rmsnorm(qkv[..., :wq].reshape(B, T, GQA_HEADS, GQA_DH), q_g)
    k = rmsnorm(qkv[..., wq:wq + wk].reshape(B, T, GQA_KV, GQA_DH), k_g)
    v = qkv[..., wq + wk:].reshape(B, T, GQA_KV, GQA_DH)
    new_k, new_v = k, v
    if rope is not None:
        q = apply_rope(q, rope)
        k = apply_rope(k, rope)
    if ctx_k is not None:
        k = jnp.concatenate([ctx_k, k], axis=1)
        v = jnp.concatenate([ctx_v, v], axis=1)
    q = q.reshape(B, T, GQA_KV, GQA_GROUP, GQA_DH)
    scale = GQA_DH ** -0.5

    def block(qb):
        s = jnp.einsum('bqkgd,bskd->bkgqs', qb, k, preferred_element_type=jnp.float32) * scale
        p = jax.nn.softmax(s, axis=-1).astype(v.dtype)
        return jnp.einsum('bkgqs,bskd->bqkgd', p, v)

    o = sweep_query_blocks(block, q)
    return o.reshape(B, T, wq) @ w_o, new_k, new_v


def moe(h, w_router, b_router, w_gu, b_gu, w_dn, b_dn):
    B, T, D = h.shape
    N = B * T
    x = h.reshape(N, D)
    logits = (x @ w_router + b_router).astype(jnp.float32)
    top_v, top_i = lax.top_k(logits, TOP_K)
    gates = jax.nn.softmax(top_v, axis=-1).astype(x.dtype)
    NK = N * TOP_K
    e_flat = top_i.reshape(NK).astype(jnp.int32)
    tok_flat = jnp.arange(NK, dtype=jnp.int32) // TOP_K
    order = jnp.argsort(e_flat)
    sorted_e = e_flat[order]
    counts = jnp.bincount(e_flat, length=N_EXPERTS).astype(jnp.int32)
    padded = (counts + EXPERT_BLOCK - 1) // EXPERT_BLOCK * EXPERT_BLOCK
    starts = jnp.cumsum(counts) - counts
    pends = jnp.cumsum(padded)
    pstarts = pends - padded
    dest = pstarts[sorted_e] + jnp.arange(NK, dtype=jnp.int32) - starts[sorted_e]
    P = -(-NK // EXPERT_BLOCK) * EXPERT_BLOCK + N_EXPERTS * EXPERT_BLOCK
    slot_tok = jnp.full((P,), N, jnp.int32).at[dest].set(tok_flat[order])
    slot_gate = jnp.zeros((P,), x.dtype).at[dest].set(gates.reshape(NK)[order])
    nb = P // EXPERT_BLOCK
    block_start = jnp.arange(nb, dtype=jnp.int32) * EXPERT_BLOCK
    block_e = jnp.minimum(jnp.searchsorted(pends, block_start, side='right'), N_EXPERTS - 1)
    xb = jnp.concatenate([x, jnp.zeros((1, D), x.dtype)], axis=0)[slot_tok].reshape(nb, EXPERT_BLOCK, D)

    def expert_block(args):
        xx, e = args
        hgu = xx @ w_gu[e] + b_gu[e]
        x_glu = jnp.minimum(hgu[:, 0::2], SWIGLU_LIMIT)
        x_lin = jnp.clip(hgu[:, 1::2], -SWIGLU_LIMIT, SWIGLU_LIMIT)
        act = x_glu * jax.nn.sigmoid(SWIGLU_ALPHA * x_glu) * (x_lin + 1)
        return act @ w_dn[e] + b_dn[e]

    yb = lax.map(expert_block, (xb, block_e))
    y = jax.ops.segment_sum(yb.reshape(P, D) * slot_gate[:, None], slot_tok, num_segments=N + 1)[:N]
    return y.reshape(B, T, D)


def run_trunk(x, cond, rope, ctx_k_diff, ctx_v_diff, ctx_k_gqa, ctx_v_gqa,
              w_ada, b_ada, norm_mix_g, norm_ffn_g,
              diff_w_qkv, diff_w_o, diff_lambda, diff_subln_g,
              gqa_w_qkv, gqa_w_o, gqa_q_norm_g, gqa_k_norm_g,
              moe_w_router, moe_b_router, moe_w_gate_up, moe_b_gate_up, moe_w_down, moe_b_down,
              final_norm_g):
    has_ctx = ctx_k_diff is not None
    kd, vd, kg, vg = [], [], [], []
    cond_act = jax.nn.silu(cond)
    for i in range(DEPTH):
        mod = (cond_act @ w_ada[i] + b_ada[i])[:, None, :]
        sh1, sc1, g1, sh2, sc2, g2 = jnp.split(mod, N_MOD, axis=-1)
        h = rmsnorm(x, norm_mix_g[i]) * (1 + sc1) + sh1
        j = i // N_MIXERS
        if i % N_MIXERS == 0:
            y, k_new, v_new = diff_attention(
                h, diff_w_qkv[j], diff_w_o[j], diff_lambda[j], diff_subln_g[j], lambda_init(i), rope,
                ctx_k_diff[:, j] if has_ctx else None, ctx_v_diff[:, j] if has_ctx else None)
            kd.append(k_new)
            vd.append(v_new)
        else:
            y, k_new, v_new = gqa_attention(
                h, gqa_w_qkv[j], gqa_w_o[j], gqa_q_norm_g[j], gqa_k_norm_g[j], rope,
                ctx_k_gqa[:, j] if has_ctx else None, ctx_v_gqa[:, j] if has_ctx else None)
            kg.append(k_new)
            vg.append(v_new)
        x = x + g1 * y
        h = rmsnorm(x, norm_ffn_g[i]) * (1 + sc2) + sh2
        x = x + g2 * moe(h, moe_w_router[i], moe_b_router[i], moe_w_gate_up[i], moe_b_gate_up[i],
                         moe_w_down[i], moe_b_down[i])
    return rmsnorm(x, final_norm_g), kd, vd, kg, vg


def setup_inputs(seed: int = 0) -> dict:
    key = jax.random.key(seed)
    ks = jax.random.split(key, 32)
    f32 = jnp.float32
    D = D_MODEL
    nrm = lambda k, shape, s: jax.random.normal(k, shape, f32) * s
    gain = lambda k, shape: 1.0 + 0.01 * jax.random.normal(k, shape, f32)
    qkv_diff = 3 * DIFF_HEADS * 2 * DIFF_DH
    qkv_gqa = (GQA_HEADS + 2 * GQA_KV) * GQA_DH
    return {
        'x_prompt': nrm(ks[0], (BATCH, SEQ, D), 1.0),
        'x_sample': nrm(ks[1], (DEC_BATCH, DEC_SEQ, D), 1.0),
        'cache_k_diff': nrm(ks[2], (DEC_BATCH, N_DIFF, PAST_LEN, DIFF_HEADS, 2 * DIFF_DH), 1.0),
        'cache_v_diff': nrm(ks[3], (DEC_BATCH, N_DIFF, PAST_LEN, DIFF_HEADS, 2 * DIFF_DH), 1.0),
        'cache_k_gqa': nrm(ks[4], (DEC_BATCH, N_GQA, PAST_LEN, GQA_KV, GQA_DH), 1.0),
        'cache_v_gqa': nrm(ks[5], (DEC_BATCH, N_GQA, PAST_LEN, GQA_KV, GQA_DH), 1.0),
        'c': nrm(ks[6], (DEC_BATCH, D), 1.0),
        'c_ctx': nrm(ks[7], (D,), 1.0),
        'w_ada': nrm(ks[8], (DEPTH, D, N_MOD * D), 0.5 * D ** -0.5),
        'b_ada': nrm(ks[9], (DEPTH, N_MOD * D), 0.01),
        'norm_mix_g': gain(ks[10], (DEPTH, D)),
        'norm_ffn_g': gain(ks[11], (DEPTH, D)),
        'diff_w_qkv': nrm(ks[12], (N_DIFF, D, qkv_diff), D ** -0.5),
        'diff_w_o': nrm(ks[13], (N_DIFF, DIFF_HEADS * 2 * DIFF_DH, D), (DIFF_HEADS * 2 * DIFF_DH) ** -0.5),
        'diff_lambda': nrm(ks[14], (N_DIFF, 4, DIFF_DH), 0.1),
        'diff_subln_g': gain(ks[15], (N_DIFF, 2 * DIFF_DH)),
        'gqa_w_qkv': nrm(ks[16], (N_GQA, D, qkv_gqa), D ** -0.5),
        'gqa_w_o': nrm(ks[17], (N_GQA, GQA_HEADS * GQA_DH, D), (GQA_HEADS * GQA_DH) ** -0.5),
        'gqa_q_norm_g': gain(ks[18], (N_GQA, GQA_DH)),
        'gqa_k_norm_g': gain(ks[19], (N_GQA, GQA_DH)),
        'moe_w_router': nrm(ks[20], (DEPTH, D, N_EXPERTS), D ** -0.5),
        'moe_b_router': nrm(ks[21], (DEPTH, N_EXPERTS), 0.01),
        'moe_w_gate_up': nrm(ks[22], (DEPTH, N_EXPERTS, D, 2 * D_FF), D ** -0.5),
        'moe_b_gate_up': nrm(ks[23], (DEPTH, N_EXPERTS, 2 * D_FF), 0.01),
        'moe_w_down': nrm(ks[24], (DEPTH, N_EXPERTS, D_FF, D), D_FF ** -0.5),
        'moe_b_down': nrm(ks[25], (DEPTH, N_EXPERTS, D), 0.01),
        'final_norm_g': gain(ks[26], (D,)),
    }


def reference(x_prompt, x_sample, cache_k_diff, cache_v_diff, cache_k_gqa, cache_v_gqa, c, c_ctx,
              w_ada, b_ada, norm_mix_g, norm_ffn_g,
              diff_w_qkv, diff_w_o, diff_lambda, diff_subln_g,
              gqa_w_qkv, gqa_w_o, gqa_q_norm_g, gqa_k_norm_g,
              moe_w_router, moe_b_router, moe_w_gate_up, moe_b_gate_up, moe_w_down, moe_b_down,
              final_norm_g):
    y_prompt, kd, vd, kg, vg = run_trunk(
        x_prompt, c_ctx[None, :], None, None, None, None, None,
        w_ada, b_ada, norm_mix_g, norm_ffn_g,
        diff_w_qkv, diff_w_o, diff_lambda, diff_subln_g,
        gqa_w_qkv, gqa_w_o, gqa_q_norm_g, gqa_k_norm_g,
        moe_w_router, moe_b_router, moe_w_gate_up, moe_b_gate_up, moe_w_down, moe_b_down,
        final_norm_g)
    new_k_diff = jnp.stack(kd, axis=1)
    new_v_diff = jnp.stack(vd, axis=1)
    new_k_gqa = jnp.stack(kg, axis=1)
    new_v_gqa = jnp.stack(vg, axis=1)
    rope = grid_rope(x_sample.shape[1], x_sample.dtype)
    y_sample, _, _, _, _ = run_trunk(
        x_sample, c, rope, cache_k_diff, cache_v_diff, cache_k_gqa, cache_v_gqa,
        w_ada, b_ada, norm_mix_g, norm_ffn_g,
        diff_w_qkv, diff_w_o, diff_lambda, diff_subln_g,
        gqa_w_qkv, gqa_w_o, gqa_q_norm_g, gqa_k_norm_g,
        moe_w_router, moe_b_router, moe_w_gate_up, moe_b_gate_up, moe_w_down, moe_b_down,
        final_norm_g)
    return (y_prompt, y_sample, new_k_diff, new_v_diff, new_k_gqa, new_v_gqa)
```

```python
import functools
import math

import jax
import jax.numpy as jnp
from jax import lax
from jax.experimental import pallas as pl
from jax.experimental.pallas import tpu as pltpu

F32 = jnp.float32
BF16 = jnp.bfloat16
I32 = jnp.int32

HEAD_DIM = 128
GROUP_LANES = 2 * HEAD_DIM
GRID_W = 64
ROPE_THETA = 10000.0
TOP_K = 4
SWIGLU_ALPHA = 1.702
SWIGLU_LIMIT = 7.0
EPS = 1e-6
N_MOD = 6
LOG2E = 1.4426950408889634

VMEM_LIMIT_BYTES = 56 * 1024 * 1024
LANES = 128
NEG_BIG = -1e30


def _cparams(sem):
    return pltpu.CompilerParams(dimension_semantics=sem, vmem_limit_bytes=VMEM_LIMIT_BYTES)


def _tiles(n_prompt, dec_seq):
    tm = 512
    while n_prompt % tm or dec_seq % tm:
        tm //= 2
    return tm


def _mod_row(m, tm, n_prompt, dec_seq):
    mp = n_prompt // tm
    return jnp.where(m < mp, 0, 1 + (m - mp) // (dec_seq // tm))


def _ada_kernel(c_ref, w_ref, b_ref, o_ref):
    c = c_ref[...]
    a = (c * jax.nn.sigmoid(c)).astype(BF16)
    w = w_ref[0].astype(BF16)
    o_ref[0] = jnp.dot(a, w, preferred_element_type=F32) + b_ref[0]


def _ada_call(cond8, w_ada, b_ada):
    depth, d, nmod = w_ada.shape
    tn = 1024 if nmod % 1024 == 0 else nmod
    return pl.pallas_call(
        _ada_kernel,
        grid=(depth, nmod // tn),
        in_specs=[
            pl.BlockSpec((8, d), lambda l, n: (0, 0)),
            pl.BlockSpec((1, d, tn), lambda l, n: (l, 0, n)),
            pl.BlockSpec((1, 1, tn), lambda l, n: (l, 0, n)),
        ],
        out_specs=pl.BlockSpec((1, 8, tn), lambda l, n: (l, 0, n)),
        out_shape=jax.ShapeDtypeStruct((depth, 8, nmod), F32),
        compiler_params=_cparams(("arbitrary", "arbitrary")),
        name="ada_mod",
    )(cond8, w_ada, b_ada.reshape(depth, 1, nmod))


def _normmod(x, g, sc, sh):
    y = x * lax.rsqrt(jnp.mean(x * x, axis=-1, keepdims=True) + EPS)
    return (y * g) * (1.0 + sc) + sh


def _normmod_kernel(x_ref, g_ref, sc_ref, sh_ref, h_ref):
    h_ref[...] = _normmod(x_ref[...], g_ref[...], sc_ref[0], sh_ref[0]).astype(BF16)


def _normmod_router_kernel(x_ref, g_ref, sc_ref, sh_ref, wh_ref, wl_ref, br_ref,
                           h_ref, ti_ref, gt_ref, *, n_exp):
    h = _normmod(x_ref[...], g_ref[...], sc_ref[0], sh_ref[0])
    hi = h.astype(BF16)
    h_ref[...] = hi
    lo = (h - hi.astype(F32)).astype(BF16)
    wh = wh_ref[...]
    lg = (jnp.dot(hi, wh, preferred_element_type=F32)
          + jnp.dot(lo, wh, preferred_element_type=F32)
          + jnp.dot(hi, wl_ref[...], preferred_element_type=F32)) + br_ref[...]
    cur = lg.T[:n_exp]
    eid = lax.broadcasted_iota(I32, cur.shape, 0)
    vals, idxs = [], []
    for _ in range(TOP_K):
        mx = jnp.max(cur, axis=0, keepdims=True)
        ix = jnp.min(jnp.where(cur == mx, eid, n_exp), axis=0, keepdims=True)
        vals.append(mx)
        idxs.append(ix)
        cur = jnp.where(eid == ix, jnp.finfo(F32).min, cur)
    ex = [jnp.exp(v - vals[0]) for v in vals]
    den = ex[0] + ex[1] + ex[2] + ex[3]
    for k in range(TOP_K):
        ti_ref[k:k + 1, :] = idxs[k]
        gt_ref[k:k + 1, :] = ex[k] / den


def _normmod_call(x, g, mod48, chunk_sc, chunk_sh, n_prompt, dec_seq, router=None):
    n, d = x.shape
    tm = _tiles(n_prompt, dec_seq)
    row = functools.partial(_mod_row, tm=tm, n_prompt=n_prompt, dec_seq=dec_seq)
    in_specs = [
        pl.BlockSpec((tm, d), lambda m: (m, 0)),
        pl.BlockSpec((1, d), lambda m: (0, 0)),
        pl.BlockSpec((1, 1, d), lambda m: (row(m) * N_MOD + chunk_sc, 0, 0)),
        pl.BlockSpec((1, 1, d), lambda m: (row(m) * N_MOD + chunk_sh, 0, 0)),
    ]
    h_spec = pl.BlockSpec((tm, d), lambda m: (m, 0))
    h_shape = jax.ShapeDtypeStruct((n, d), BF16)
    if router is None:
        return pl.pallas_call(
            _normmod_kernel, grid=(n // tm,), in_specs=in_specs, out_specs=h_spec, out_shape=h_shape,
            compiler_params=_cparams(("arbitrary",)), name="normmod",
        )(x, g.reshape(1, d), mod48, mod48)
    wr_hi, wr_lo, br, n_exp = router
    in_specs += [
        pl.BlockSpec((d, LANES), lambda m: (0, 0)),
        pl.BlockSpec((d, LANES), lambda m: (0, 0)),
        pl.BlockSpec((1, LANES), lambda m: (0, 0)),
    ]
    kv_spec = pl.BlockSpec((TOP_K, tm), lambda m: (0, m))
    return pl.pallas_call(
        functools.partial(_normmod_router_kernel, n_exp=n_exp),
        grid=(n // tm,), in_specs=in_specs,
        out_specs=[h_spec, kv_spec, kv_spec],
        out_shape=[h_shape, jax.ShapeDtypeStruct((TOP_K, n), I32), jax.ShapeDtypeStruct((TOP_K, n), F32)],
        compiler_params=_cparams(("arbitrary",)), name="normmod_router",
    )(x, g.reshape(1, d), mod48, mod48, wr_hi, wr_lo, br)


def _proj_kernel(*refs, norm, rope, scale, f32_out, heads):
    it = iter(refs)
    x_ref, w_ref = next(it), next(it)
    gain_ref = next(it) if norm else None
    cos_ref, sin_ref = (next(it), next(it)) if rope else (None, None)
    o_ref = next(it)
    o32_ref = next(it) if f32_out else None
    acc = jnp.dot(x_ref[...], w_ref[...], preferred_element_type=F32)
    if rope:
        cos, sin = cos_ref[...], sin_ref[...]
        lane = lax.broadcasted_iota(I32, cos.shape, 1)
        first_half = (lane % (HEAD_DIM // 2)) < (HEAD_DIM // 4)
    for j in range(heads):
        seg = slice(j * HEAD_DIM, (j + 1) * HEAD_DIM)
        y = acc[:, seg]
        if norm:
            y = (y * lax.rsqrt(jnp.mean(y * y, axis=-1, keepdims=True) + EPS)) * gain_ref[...]
        if f32_out:
            o32_ref[:, seg] = y
        if rope:
            r = jnp.where(first_half, pltpu.roll(y, HEAD_DIM - HEAD_DIM // 4, 1), pltpu.roll(y, HEAD_DIM // 4, 1))
            y = y * cos + r * sin
        if scale != 1.0:
            y = y * scale
        o_ref[:, seg] = y.astype(BF16)


def _proj_call(h, w, *, row_off, n_rows, col_off, n_cols, tm, gain=None, rope=None, scale=1.0,
               f32_out=False, name="proj"):
    d = h.shape[1]
    tn = 512 if n_cols % 512 == 0 else n_cols
    mo, no = row_off // tm, col_off // tn
    in_specs = [
        pl.BlockSpec((tm, d), lambda m, n: (m + mo, 0)),
        pl.BlockSpec((d, tn), lambda m, n: (0, n + no)),
    ]
    args = [h, w]
    if gain is not None:
        in_specs.append(pl.BlockSpec((1, HEAD_DIM), lambda m, n: (0, 0)))
        args.append(gain.reshape(1, HEAD_DIM))
    if rope is not None:
        cos, sin = rope
        tb = cos.shape[0] // tm
        in_specs += [pl.BlockSpec((tm, HEAD_DIM), lambda m, n: (m % tb, 0))] * 2
        args += [cos, sin]
    out_specs = [pl.BlockSpec((tm, tn), lambda m, n: (m, n))]
    out_shape = [jax.ShapeDtypeStruct((n_rows, n_cols), BF16)]
    if f32_out:
        out_specs.append(pl.BlockSpec((tm, tn), lambda m, n: (m, n)))
        out_shape.append(jax.ShapeDtypeStruct((n_rows, n_cols), F32))
    return pl.pallas_call(
        functools.partial(_proj_kernel, norm=gain is not None, rope=rope is not None, scale=scale,
                          f32_out=f32_out, heads=tn // HEAD_DIM),
        grid=(n_rows // tm, n_cols // tn), in_specs=in_specs, out_specs=out_specs, out_shape=out_shape,
        compiler_params=_cparams(("arbitrary", "arbitrary")), name=name,
    )(*args)


def _attn_kernel(*refs, diff, n_chunks, tk, lam_init):
    if diff:
        lam_ref, subg_ref, q_ref, k_ref, v_ref, o_ref = refs
    else:
        q_ref, k_ref, v_ref, o_ref = refs
    tq = q_ref.shape[1]
    dv = v_ref.shape[2]
    outs = []
    for m in range(2):
        q = q_ref[0, :, m * HEAD_DIM:(m + 1) * HEAD_DIM]

        def body(c, carry, q=q, m=m):
            mx, l, acc = carry
            off = pl.multiple_of(c * tk, tk)
            if diff:
                kc = k_ref[0, pl.ds(off, tk), m * HEAD_DIM:(m + 1) * HEAD_DIM]
            else:
                kc = k_ref[0, pl.ds(off, tk), :]
            vc = v_ref[0, pl.ds(off, tk), :]
            s = lax.dot_general(q, kc, (((1,), (1,)), ((), ())), preferred_element_type=F32)
            mn = jnp.maximum(mx, jnp.max(s, axis=-1, keepdims=True))
            alpha = jnp.exp2(mx - mn)
            p = jnp.exp2(s - mn)
            l = alpha * l + jnp.sum(p, axis=-1, keepdims=True)
            acc = alpha * acc + jnp.dot(p.astype(BF16), vc, preferred_element_type=F32)
            return mn, l, acc

        init = (jnp.full((tq, 1), NEG_BIG, F32), jnp.zeros((tq, 1), F32), jnp.zeros((tq, dv), F32))
        _, l, acc = lax.fori_loop(0, n_chunks, body, init)
        outs.append(acc / l)
    if diff:
        lp = lam_ref[...]
        lam = (jnp.exp(jnp.sum(lp[0:1] * lp[1:2], axis=-1, keepdims=True))
               - jnp.exp(jnp.sum(lp[2:3] * lp[3:4], axis=-1, keepdims=True)) + lam_init)
        o = outs[0] - lam * outs[1]
        o = (o * lax.rsqrt(jnp.mean(o * o, axis=-1, keepdims=True) + EPS)) * subg_ref[...]
        o_ref[0] = (o * (1.0 - lam_init)).astype(BF16)
    else:
        o_ref[0, :, 0:HEAD_DIM] = outs[0].astype(BF16)
        o_ref[0, :, HEAD_DIM:2 * HEAD_DIM] = outs[1].astype(BF16)


def _attn_call(q, k, v, *, diff, lam_p=None, subln_g=None, lam_init=0.0, name="attn"):
    b, t, d = q.shape
    s = k.shape[1]
    groups = d // GROUP_LANES
    kw, vw = k.shape[2] // groups, v.shape[2] // groups
    tq = min(t, 256)
    tk = 512
    while s % tk:
        tk //= 2
    assert t % tq == 0 and tk >= LANES
    in_specs = [
        pl.BlockSpec((1, tq, GROUP_LANES), lambda bi, g, qi: (bi, qi, g)),
        pl.BlockSpec((1, s, kw), lambda bi, g, qi: (bi, 0, g)),
        pl.BlockSpec((1, s, vw), lambda bi, g, qi: (bi, 0, g)),
    ]
    args = [q, k, v]
    if diff:
        in_specs = [pl.BlockSpec((4, HEAD_DIM), lambda bi, g, qi: (0, 0)),
                    pl.BlockSpec((1, GROUP_LANES), lambda bi, g, qi: (0, 0))] + in_specs
        args = [lam_p, subln_g.reshape(1, GROUP_LANES)] + args
    return pl.pallas_call(
        functools.partial(_attn_kernel, diff=diff, n_chunks=s // tk, tk=tk, lam_init=lam_init),
        grid=(b, groups, t // tq), in_specs=in_specs,
        out_specs=pl.BlockSpec((1, tq, GROUP_LANES), lambda bi, g, qi: (bi, qi, g)),
        out_shape=jax.ShapeDtypeStruct((b, t, d), BF16),
        compiler_params=_cparams(("arbitrary", "arbitrary", "arbitrary")), name=name,
    )(*args)


def _oproj_kernel(o_ref, w_ref, g_ref, x_ref, out_ref):
    y = jnp.dot(o_ref[...], w_ref[...], preferred_element_type=F32)
    out_ref[...] = x_ref[...] + g_ref[0] * y


def _oproj_call(o, w, mod48, x, *, row_off, tm, n_prompt, dec_seq):
    n_rows, d_in = o.shape
    d = x.shape[1]
    tn = 512 if d % 512 == 0 else d
    mo = row_off // tm
    row = functools.partial(_mod_row, tm=tm, n_prompt=n_prompt, dec_seq=dec_seq)
    return pl.pallas_call(
        _oproj_kernel,
        grid=(n_rows // tm, d // tn),
        in_specs=[
            pl.BlockSpec((tm, d_in), lambda m, n: (m, 0)),
            pl.BlockSpec((d_in, tn), lambda m, n: (0, n)),
            pl.BlockSpec((1, 1, tn), lambda m, n: (row(m + mo) * N_MOD + 2, 0, n)),
            pl.BlockSpec((tm, tn), lambda m, n: (m + mo, n)),
        ],
        out_specs=pl.BlockSpec((tm, tn), lambda m, n: (m + mo, n)),
        out_shape=jax.ShapeDtypeStruct(x.shape, F32),
        input_output_aliases={3: 0},
        compiler_params=_cparams(("arbitrary", "arbitrary")), name="oproj_resid",
    )(o, w, mod48, x)


def _moe_block(n_slots, n_exp):
    bm = 512
    while bm > 128 and n_slots // n_exp < 2 * bm:
        bm //= 2
    return bm


def _route_meta(topi, n_exp, bm):
    k, n = topi.shape
    nk = n * k
    e_flat = topi.T.reshape(nk)
    onehot = (e_flat[:, None] == jnp.arange(n_exp, dtype=I32)[None, :]).astype(I32)
    csum = jnp.cumsum(onehot, axis=0)
    rank = jnp.take_along_axis(csum, e_flat[:, None], axis=1)[:, 0] - 1
    counts = csum[-1]
    padded = (counts + bm - 1) // bm * bm
    pends = jnp.cumsum(padded)
    pstarts = pends - padded
    pos = pstarts[e_flat] + rank
    p_total = -(-nk // bm) * bm + n_exp * bm
    slot_tok = jnp.zeros((p_total,), I32).at[pos].set(jnp.arange(nk, dtype=I32) // k)
    nb = p_total // bm
    n_used = pends[-1] // bm
    blk = jnp.arange(nb, dtype=I32)
    be = jnp.minimum(jnp.searchsorted(pends, blk * bm, side="right").astype(I32), n_exp - 1)
    be = jnp.where(blk < n_used, be, be[jnp.maximum(n_used - 1, 0)])
    return slot_tok, pos.reshape(n, k), be, n_used.reshape(1).astype(I32)


def _gather_kernel(nu_ref, idx_ref, h_hbm, xs_hbm, idx_smem, isem, sem, *, rows, bm, sub):
    i = pl.program_id(0)
    base = i * rows

    @pl.when(base < nu_ref[0] * bm)
    def _():
        icp = pltpu.make_async_copy(idx_ref.at[0, 0], idx_smem, isem)
        icp.start()
        icp.wait()

        def issue(r, c):
            src = pl.multiple_of(idx_smem[r] * sub, sub)
            dst = pl.multiple_of((base + r) * sub, sub)
            pltpu.make_async_copy(h_hbm.at[pl.ds(src, sub)], xs_hbm.at[pl.ds(dst, sub)], sem).start()
            return c

        lax.fori_loop(0, rows, issue, 0, unroll=8)
        dst0 = pl.multiple_of(base * sub, sub)
        pltpu.make_async_copy(h_hbm.at[pl.ds(0, rows * sub)], xs_hbm.at[pl.ds(dst0, rows * sub)], sem).wait()


def _gather_call(h2, slot_tok, n_used, bm, sub):
    p_total = slot_tok.shape[0]
    rows = bm
    steps = p_total // rows
    return pl.pallas_call(
        functools.partial(_gather_kernel, rows=rows, bm=bm, sub=sub),
        grid_spec=pltpu.PrefetchScalarGridSpec(
            num_scalar_prefetch=1, grid=(steps,),
            in_specs=[pl.BlockSpec((1, 1, rows), lambda i, nu: (i, 0, 0)),
                      pl.BlockSpec(memory_space=pl.ANY)],
            out_specs=pl.BlockSpec(memory_space=pl.ANY),
            scratch_shapes=[pltpu.SMEM((rows,), I32), pltpu.SemaphoreType.DMA, pltpu.SemaphoreType.DMA],
        ),
        out_shape=jax.ShapeDtypeStruct((p_total * sub, LANES), h2.dtype),
        compiler_params=_cparams(("arbitrary",)), name="moe_gather",
    )(n_used, slot_tok.reshape(steps, 1, rows), h2)


def _gmm1_kernel(be_ref, nu_ref, x_ref, wg_ref, wl_ref, bg_ref, bl_ref, o_ref):
    b = pl.program_id(1)

    @pl.when(b < nu_ref[0])
    def _():
        x = x_ref[...]
        g = jnp.dot(x, wg_ref[0], preferred_element_type=F32) + bg_ref[0]
        u = jnp.dot(x, wl_ref[0], preferred_element_type=F32) + bl_ref[0]
        g = jnp.minimum(g, SWIGLU_LIMIT)
        u = jnp.clip(u, -SWIGLU_LIMIT, SWIGLU_LIMIT)
        o_ref[...] = (g * jax.nn.sigmoid(SWIGLU_ALPHA * g) * (u + 1.0)).astype(BF16)

    @pl.when(b >= nu_ref[0])
    def _():
        o_ref[...] = jnp.zeros_like(o_ref)


def _gmm1_call(xs, wg, wl, bg, bl, be, n_used, bm):
    p_total, d = xs.shape
    n_exp, _, f = wg.shape
    tf = 512 if f % 512 == 0 else f
    nb = p_total // bm
    wspec = pl.BlockSpec((1, d, tf), lambda n, b, be, nu: (be[b], 0, n))
    bspec = pl.BlockSpec((1, 1, tf), lambda n, b, be, nu: (be[b], 0, n))
    return pl.pallas_call(
        _gmm1_kernel,
        grid_spec=pltpu.PrefetchScalarGridSpec(
            num_scalar_prefetch=2, grid=(f // tf, nb),
            in_specs=[pl.BlockSpec((bm, d), lambda n, b, be, nu: (b, 0)), wspec, wspec, bspec, bspec],
            out_specs=pl.BlockSpec((bm, tf), lambda n, b, be, nu: (b, n)),
        ),
        out_shape=jax.ShapeDtypeStruct((p_total, f), BF16),
        compiler_params=_cparams(("arbitrary", "arbitrary")), name="moe_gate_up",
    )(be, n_used, xs, wg, wl, bg, bl)


def _gmm2_kernel(be_ref, nu_ref, a_ref, w_ref, b_ref, o_ref):
    b = pl.program_id(1)

    @pl.when(b < nu_ref[0])
    def _():
        o_ref[...] = jnp.dot(a_ref[...], w_ref[0], preferred_element_type=F32) + b_ref[0]

    @pl.when(b >= nu_ref[0])
    def _():
        o_ref[...] = jnp.zeros_like(o_ref)


def _gmm2_call(act, wd, bd, be, n_used, bm):
    p_total, f = act.shape
    n_exp, _, d = wd.shape
    tn = 1024 if d % 1024 == 0 else d
    nb = p_total // bm
    return pl.pallas_call(
        _gmm2_kernel,
        grid_spec=pltpu.PrefetchScalarGridSpec(
            num_scalar_prefetch=2, grid=(d // tn, nb),
            in_specs=[pl.BlockSpec((bm, f), lambda n, b, be, nu: (b, 0)),
                      pl.BlockSpec((1, f, tn), lambda n, b, be, nu: (be[b], 0, n)),
                      pl.BlockSpec((1, 1, tn), lambda n, b, be, nu: (be[b], 0, n))],
            out_specs=pl.BlockSpec((bm, tn), lambda n, b, be, nu: (b, n)),
        ),
        out_shape=jax.ShapeDtypeStruct((p_total, d), F32),
        compiler_params=_cparams(("arbitrary", "arbitrary")), name="moe_down",
    )(be, n_used, act, wd, bd)


def _combine_kernel(pos_ref, gate_ref, g2_ref, x_ref, y_hbm, out_ref, pos_smem, buf, isem, sem, *, tc, sub):
    icp = pltpu.make_async_copy(pos_ref.at[0, 0], pos_smem, isem)
    icp.start()
    icp.wait()
    n_slots = TOP_K * tc

    def issue(r, c):
        src = pl.multiple_of(pos_smem[r] * sub, sub)
        dst = pl.multiple_of(r * sub, sub)
        pltpu.make_async_copy(y_hbm.at[pl.ds(src, sub)], buf.at[pl.ds(dst, sub)], sem).start()
        return c

    lax.fori_loop(0, n_slots, issue, 0, unroll=8)
    pltpu.make_async_copy(y_hbm.at[pl.ds(0, n_slots * sub)], buf, sem).wait()
    gates = gate_ref[...]
    g2 = g2_ref[0]
    for s in range(sub):
        seg = slice(s * LANES, (s + 1) * LANES)
        acc = None
        for k in range(TOP_K):
            rows = buf[pl.ds(k * sub + s, tc, stride=TOP_K * sub), :]
            term = gates[:, k:k + 1] * rows
            acc = term if acc is None else acc + term
        out_ref[:, seg] = x_ref[:, seg] + g2[:, seg] * acc


def _combine_call(x, y2, pos, gates, mod48, sub, *, n_prompt, dec_seq):
    n, d = x.shape
    tc = 128
    steps = n // tc
    row = functools.partial(_mod_row, tm=tc, n_prompt=n_prompt, dec_seq=dec_seq)
    return pl.pallas_call(
        functools.partial(_combine_kernel, tc=tc, sub=sub),
        grid=(steps,),
        in_specs=[
            pl.BlockSpec((1, 1, TOP_K * tc), lambda i: (i, 0, 0)),
            pl.BlockSpec((tc, TOP_K), lambda i: (i, 0)),
            pl.BlockSpec((1, 1, d), lambda i: (row(i) * N_MOD + 5, 0, 0)),
            pl.BlockSpec((tc, d), lambda i: (i, 0)),
            pl.BlockSpec(memory_space=pl.ANY),
        ],
        out_specs=pl.BlockSpec((tc, d), lambda i: (i, 0)),
        out_shape=jax.ShapeDtypeStruct(x.shape, F32),
        scratch_shapes=[pltpu.SMEM((TOP_K * tc,), I32), pltpu.VMEM((TOP_K * tc * sub, LANES), F32),
                        pltpu.SemaphoreType.DMA, pltpu.SemaphoreType.DMA],
        input_output_aliases={3: 0},
        compiler_params=_cparams(("arbitrary",)), name="moe_combine",
    )(pos.reshape(steps, 1, TOP_K * tc), gates, mod48, x, y2)


def _final_kernel(x_ref, g_ref, o_ref):
    x = x_ref[...]
    o_ref[...] = (x * lax.rsqrt(jnp.mean(x * x, axis=-1, keepdims=True) + EPS)) * g_ref[...]


def _final_call(x, g, *, row_off, n_rows, tm):
    d = x.shape[1]
    mo = row_off // tm
    return pl.pallas_call(
        _final_kernel, grid=(n_rows // tm,),
        in_specs=[pl.BlockSpec((tm, d), lambda m: (m + mo, 0)), pl.BlockSpec((1, d), lambda m: (0, 0))],
        out_specs=pl.BlockSpec((tm, d), lambda m: (m, 0)),
        out_shape=jax.ShapeDtypeStruct((n_rows, d), F32),
        compiler_params=_cparams(("arbitrary",)), name="final_norm",
    )(x, g.reshape(1, d))


def _rope_tables(t):
    rows = t // GRID_W
    row = jnp.repeat(jnp.arange(rows), GRID_W).astype(F32)
    col = jnp.tile(jnp.arange(GRID_W), rows).astype(F32)
    nf = HEAD_DIM // 4
    inv = ROPE_THETA ** (-jnp.arange(nf, dtype=F32) / nf)
    ar = row[:, None] * inv
    ac = col[:, None] * inv
    ang = jnp.concatenate([ar, ar, ac, ac], axis=-1)
    sign = jnp.concatenate([-jnp.ones((nf,), F32), jnp.ones((nf,), F32)] * 2)
    return jnp.cos(ang), jnp.sin(ang) * sign


def _lambda_init(i):
    return 0.8 - 0.6 * math.exp(-0.3 * i)


def kernel(x_prompt, x_sample, cache_k_diff, cache_v_diff, cache_k_gqa, cache_v_gqa, c, c_ctx, w_ada, b_ada, norm_mix_g, norm_ffn_g, diff_w_qkv, diff_w_o, diff_lambda, diff_subln_g, gqa_w_qkv, gqa_w_o, gqa_q_norm_g, gqa_k_norm_g, moe_w_router, moe_b_router, moe_w_gate_up, moe_b_gate_up, moe_w_down, moe_b_down, final_norm_g):
    bp, tp, d = x_prompt.shape
    bs, ts, _ = x_sample.shape
    depth = w_ada.shape[0]
    n_exp = moe_w_router.shape[-1]
    d_ff = moe_w_down.shape[2]
    past = cache_k_diff.shape[2]
    n_prompt, n_sample = bp * tp, bs * ts
    n = n_prompt + n_sample
    tm = _tiles(n_prompt, ts)
    w_diff = d
    wq, wk = d, d // 2
    qscale = HEAD_DIM ** -0.5 * LOG2E

    x = jnp.concatenate([x_prompt.reshape(n_prompt, d), x_sample.reshape(n_sample, d)], axis=0)
    cond8 = jnp.zeros((8, d), F32).at[0].set(c_ctx).at[1:1 + bs].set(c)
    mod = _ada_call(cond8, w_ada, b_ada)
    rope = _rope_tables(ts)
    bm = _moe_block(n * TOP_K, n_exp)

    kd, vd, kg, vg = [], [], [], []
    for i in range(depth):
        mod48 = mod[i].reshape(8 * N_MOD, 1, d)
        h = _normmod_call(x, norm_mix_g[i], mod48, 1, 0, n_prompt, ts)
        j = i // 2
        if i % 2 == 0:
            w = diff_w_qkv[j].astype(BF16)
            pk = dict(row_off=0, n_rows=n_prompt, n_cols=w_diff, tm=tm)
            sk = dict(row_off=n_prompt, n_rows=n_sample, n_cols=w_diff, tm=tm)
            (q_p,) = _proj_call(h, w, col_off=0, scale=qscale, name="diff_q_p", **pk)
            k_p, k32 = _proj_call(h, w, col_off=w_diff, f32_out=True, name="diff_k_p", **pk)
            v_p, v32 = _proj_call(h, w, col_off=2 * w_diff, f32_out=True, name="diff_v_p", **pk)
            (q_s,) = _proj_call(h, w, col_off=0, rope=rope, scale=qscale, name="diff_q_s", **sk)
            (k_s,) = _proj_call(h, w, col_off=w_diff, rope=rope, name="diff_k_s", **sk)
            (v_s,) = _proj_call(h, w, col_off=2 * w_diff, name="diff_v_s", **sk)
            kd.append(k32.reshape(bp, tp, d // GROUP_LANES, GROUP_LANES))
            vd.append(v32.reshape(bp, tp, d // GROUP_LANES, GROUP_LANES))
            ck = cache_k_diff[:, j].reshape(bs, past, w_diff).astype(BF16)
            cv = cache_v_diff[:, j].reshape(bs, past, w_diff).astype(BF16)
            ak = dict(diff=True, lam_p=diff_lambda[j], subln_g=diff_subln_g[j], lam_init=_lambda_init(i))
            kc, vc = w_diff, w_diff
            w_o = diff_w_o[j]
        else:
            w = gqa_w_qkv[j].astype(BF16)
            pk = dict(row_off=0, n_rows=n_prompt, tm=tm)
            sk = dict(row_off=n_prompt, n_rows=n_sample, tm=tm)
            qg, kgain = gqa_q_norm_g[j], gqa_k_norm_g[j]
            (q_p,) = _proj_call(h, w, col_off=0, n_cols=wq, gain=qg, scale=qscale, name="gqa_q_p", **pk)
            k_p, k32 = _proj_call(h, w, col_off=wq, n_cols=wk, gain=kgain, f32_out=True, name="gqa_k_p", **pk)
            v_p, v32 = _proj_call(h, w, col_off=wq + wk, n_cols=wk, f32_out=True, name="gqa_v_p", **pk)
            (q_s,) = _proj_call(h, w, col_off=0, n_cols=wq, gain=qg, rope=rope, scale=qscale, name="gqa_q_s", **sk)
            (k_s,) = _proj_call(h, w, col_off=wq, n_cols=wk, gain=kgain, rope=rope, name="gqa_k_s", **sk)
            (v_s,) = _proj_call(h, w, col_off=wq + wk, n_cols=wk, name="gqa_v_s", **sk)
            kg.append(k32.reshape(bp, tp, wk // HEAD_DIM, HEAD_DIM))
            vg.append(v32.reshape(bp, tp, wk // HEAD_DIM, HEAD_DIM))
            ck = cache_k_gqa[:, j].reshape(bs, past, wk).astype(BF16)
            cv = cache_v_gqa[:, j].reshape(bs, past, wk).astype(BF16)
            ak = dict(diff=False)
            kc, vc = wk, wk
            w_o = gqa_w_o[j]
        o_p = _attn_call(q_p.reshape(bp, tp, d), k_p.reshape(bp, tp, kc), v_p.reshape(bp, tp, vc),
                         name="attn_p", **ak)
        k_all = jnp.concatenate([ck, k_s.reshape(bs, ts, kc)], axis=1)
        v_all = jnp.concatenate([cv, v_s.reshape(bs, ts, vc)], axis=1)
        o_s = _attn_call(q_s.reshape(bs, ts, d), k_all, v_all, name="attn_s", **ak)
        w_o = w_o.astype(BF16)
        ok = dict(tm=tm, n_prompt=n_prompt, dec_seq=ts)
        x = _oproj_call(o_p.reshape(n_prompt, d), w_o, mod48, x, row_off=0, **ok)
        x = _oproj_call(o_s.reshape(n_sample, d), w_o, mod48, x, row_off=n_prompt, **ok)

        wr = jnp.zeros((d, LANES), F32).at[:, :n_exp].set(moe_w_router[i])
        wr_hi = wr.astype(BF16)
        wr_lo = (wr - wr_hi.astype(F32)).astype(BF16)
        br = jnp.zeros((1, LANES), F32).at[0, :n_exp].set(moe_b_router[i])
        h, topi, gates = _normmod_call(x, norm_ffn_g[i], mod48, 4, 3, n_prompt, ts,
                                       router=(wr_hi, wr_lo, br, n_exp))
        slot_tok, pos, be, n_used = _route_meta(topi, n_exp, bm)
        sub = d // LANES
        xs2 = _gather_call(h.reshape(n * sub, LANES), slot_tok, n_used, bm, sub)
        xs = xs2.reshape(xs2.shape[0] // sub, d)
        w_gu = moe_w_gate_up[i]
        wg = w_gu[..., 0::2].astype(BF16)
        wl = w_gu[..., 1::2].astype(BF16)
        b_gu = moe_b_gate_up[i]
        act = _gmm1_call(xs, wg, wl, b_gu[:, None, 0::2], b_gu[:, None, 1::2], be, n_used, bm)
        yb = _gmm2_call(act, moe_w_down[i].astype(BF16), moe_b_down[i][:, None, :], be, n_used, bm)
        x = _combine_call(x, yb.reshape(yb.shape[0] * sub, LANES), pos.reshape(-1), gates.T, mod48, sub,
                          n_prompt=n_prompt, dec_seq=ts)

    y_p = _final_call(x, final_norm_g, row_off=0, n_rows=n_prompt, tm=tm)
    y_s = _final_call(x, final_norm_g, row_off=n_prompt, n_rows=n_sample, tm=tm)
    return (y_p.reshape(bp, tp, d), y_s.reshape(bs, ts, d),
            jnp.stack(kd, axis=1), jnp.stack(vd, axis=1), jnp.stack(kg, axis=1), jnp.stack(vg, axis=1))
```

```python
import functools
import math

import jax
import jax.numpy as jnp
from jax import lax
from jax.experimental import pallas as pl
from jax.experimental.pallas import tpu as pltpu

F32 = jnp.float32
BF16 = jnp.bfloat16
I32 = jnp.int32

HEAD_DIM = 128
GROUP_LANES = 2 * HEAD_DIM
GRID_W = 64
ROPE_THETA = 10000.0
TOP_K = 4
SWIGLU_ALPHA = 1.702
SWIGLU_LIMIT = 7.0
EPS = 1e-6
N_MOD = 6
LOG2E = 1.4426950408889634

VMEM_LIMIT_BYTES = 56 * 1024 * 1024
LANES = 128
NEG_BIG = -1e30


def _cparams(sem):
    return pltpu.CompilerParams(dimension_semantics=sem, vmem_limit_bytes=VMEM_LIMIT_BYTES)


def _tiles(n_prompt, dec_seq):
    tm = 512
    while n_prompt % tm or dec_seq % tm:
        tm //= 2
    return tm


def _mod_row(m, tm, n_prompt, dec_seq):
    mp = n_prompt // tm
    return jnp.where(m < mp, 0, 1 + (m - mp) // (dec_seq // tm))


def _ada_kernel(c_ref, w_ref, b_ref, o_ref):
    c = c_ref[...]
    a = (c * jax.nn.sigmoid(c)).astype(BF16)
    w = w_ref[0].astype(BF16)
    o_ref[0] = jnp.dot(a, w, preferred_element_type=F32) + b_ref[0]


def _ada_call(cond8, w_ada, b_ada):
    depth, d, nmod = w_ada.shape
    tn = 1024 if nmod % 1024 == 0 else nmod
    return pl.pallas_call(
        _ada_kernel,
        grid=(depth, nmod // tn),
        in_specs=[
            pl.BlockSpec((8, d), lambda l, n: (0, 0)),
            pl.BlockSpec((1, d, tn), lambda l, n: (l, 0, n)),
            pl.BlockSpec((1, 1, tn), lambda l, n: (l, 0, n)),
        ],
        out_specs=pl.BlockSpec((1, 8, tn), lambda l, n: (l, 0, n)),
        out_shape=jax.ShapeDtypeStruct((depth, 8, nmod), F32),
        compiler_params=_cparams(("arbitrary", "arbitrary")),
        name="ada_mod",
    )(cond8, w_ada, b_ada.reshape(depth, 1, nmod))


def _normmod(x, g, sc, sh):
    y = x * lax.rsqrt(jnp.mean(x * x, axis=-1, keepdims=True) + EPS)
    return (y * g) * (1.0 + sc) + sh


def _normmod_kernel(x_ref, g_ref, sc_ref, sh_ref, h_ref):
    h_ref[...] = _normmod(x_ref[...], g_ref[...], sc_ref[0], sh_ref[0]).astype(BF16)


def _normmod_router_kernel(x_ref, g_ref, sc_ref, sh_ref, wh_ref, wl_ref, br_ref,
                           h_ref, ti_ref, gt_ref, *, n_exp):
    h = _normmod(x_ref[...], g_ref[...], sc_ref[0], sh_ref[0])
    h_ref[...] = h
    hi = h.astype(BF16)
    lo = (h - hi.astype(F32)).astype(BF16)
    wh = wh_ref[...]
    lg = (jnp.dot(hi, wh, preferred_element_type=F32)
          + jnp.dot(lo, wh, preferred_element_type=F32)
          + jnp.dot(hi, wl_ref[...], preferred_element_type=F32)) + br_ref[...]
    cur = lg.T[:n_exp]
    eid = lax.broadcasted_iota(I32, cur.shape, 0)
    vals, idxs = [], []
    for _ in range(TOP_K):
        mx = jnp.max(cur, axis=0, keepdims=True)
        ix = jnp.min(jnp.where(cur == mx, eid, n_exp), axis=0, keepdims=True)
        vals.append(mx)
        idxs.append(ix)
        cur = jnp.where(eid == ix, jnp.finfo(F32).min, cur)
    ex = [jnp.exp(v - vals[0]) for v in vals]
    den = ex[0] + ex[1] + ex[2] + ex[3]
    for k in range(TOP_K):
        ti_ref[k:k + 1, :] = idxs[k]
        gt_ref[k:k + 1, :] = ex[k] / den


def _normmod_call(x, g, mod48, chunk_sc, chunk_sh, n_prompt, dec_seq, router=None):
    n, d = x.shape
    tm = _tiles(n_prompt, dec_seq)
    row = functools.partial(_mod_row, tm=tm, n_prompt=n_prompt, dec_seq=dec_seq)
    in_specs = [
        pl.BlockSpec((tm, d), lambda m: (m, 0)),
        pl.BlockSpec((1, d), lambda m: (0, 0)),
        pl.BlockSpec((1, 1, d), lambda m: (row(m) * N_MOD + chunk_sc, 0, 0)),
        pl.BlockSpec((1, 1, d), lambda m: (row(m) * N_MOD + chunk_sh, 0, 0)),
    ]
    h_spec = pl.BlockSpec((tm, d), lambda m: (m, 0))
    h_shape = jax.ShapeDtypeStruct((n, d), BF16)
    if router is None:
        return pl.pallas_call(
            _normmod_kernel, grid=(n // tm,), in_specs=in_specs, out_specs=h_spec, out_shape=h_shape,
            compiler_params=_cparams(("arbitrary",)), name="normmod",
        )(x, g.reshape(1, d), mod48, mod48)
    wr_hi, wr_lo, br, n_exp = router
    in_specs += [
        pl.BlockSpec((d, LANES), lambda m: (0, 0)),
        pl.BlockSpec((d, LANES), lambda m: (0, 0)),
        pl.BlockSpec((1, LANES), lambda m: (0, 0)),
    ]
    kv_spec = pl.BlockSpec((TOP_K, tm), lambda m: (0, m))
    return pl.pallas_call(
        functools.partial(_normmod_router_kernel, n_exp=n_exp),
        grid=(n // tm,), in_specs=in_specs,
        out_specs=[h_spec, kv_spec, kv_spec],
        out_shape=[jax.ShapeDtypeStruct((n, d), F32), jax.ShapeDtypeStruct((TOP_K, n), I32),
                   jax.ShapeDtypeStruct((TOP_K, n), F32)],
        compiler_params=_cparams(("arbitrary",)), name="normmod_router",
    )(x, g.reshape(1, d), mod48, mod48, wr_hi, wr_lo, br)


def _proj_kernel(*refs, norm, rope, scale, f32_out, heads):
    it = iter(refs)
    x_ref, w_ref = next(it), next(it)
    gain_ref = next(it) if norm else None
    cos_ref, sin_ref = (next(it), next(it)) if rope else (None, None)
    o_ref = next(it)
    o32_ref = next(it) if f32_out else None
    acc = jnp.dot(x_ref[...], w_ref[...], preferred_element_type=F32)
    if rope:
        cos, sin = cos_ref[...], sin_ref[...]
        lane = lax.broadcasted_iota(I32, cos.shape, 1)
        first_half = (lane % (HEAD_DIM // 2)) < (HEAD_DIM // 4)
    for j in range(heads):
        seg = slice(j * HEAD_DIM, (j + 1) * HEAD_DIM)
        y = acc[:, seg]
        if norm:
            y = (y * lax.rsqrt(jnp.mean(y * y, axis=-1, keepdims=True) + EPS)) * gain_ref[...]
        if f32_out:
            o32_ref[:, seg] = y
        if rope:
            r = jnp.where(first_half, pltpu.roll(y, HEAD_DIM - HEAD_DIM // 4, 1), pltpu.roll(y, HEAD_DIM // 4, 1))
            y = y * cos + r * sin
        if scale != 1.0:
            y = y * scale
        o_ref[:, seg] = y.astype(BF16)


def _proj_call(h, w, *, row_off, n_rows, col_off, n_cols, tm, gain=None, rope=None, scale=1.0,
               f32_out=False, name="proj"):
    d = h.shape[1]
    tn = 512 if n_cols % 512 == 0 else n_cols
    mo, no = row_off // tm, col_off // tn
    in_specs = [
        pl.BlockSpec((tm, d), lambda m, n: (m + mo, 0)),
        pl.BlockSpec((d, tn), lambda m, n: (0, n + no)),
    ]
    args = [h, w]
    if gain is not None:
        in_specs.append(pl.BlockSpec((1, HEAD_DIM), lambda m, n: (0, 0)))
        args.append(gain.reshape(1, HEAD_DIM))
    if rope is not None:
        cos, sin = rope
        tb = cos.shape[0] // tm
        in_specs += [pl.BlockSpec((tm, HEAD_DIM), lambda m, n: (m % tb, 0))] * 2
        args += [cos, sin]
    out_specs = [pl.BlockSpec((tm, tn), lambda m, n: (m, n))]
    out_shape = [jax.ShapeDtypeStruct((n_rows, n_cols), BF16)]
    if f32_out:
        out_specs.append(pl.BlockSpec((tm, tn), lambda m, n: (m, n)))
        out_shape.append(jax.ShapeDtypeStruct((n_rows, n_cols), F32))
    return pl.pallas_call(
        functools.partial(_proj_kernel, norm=gain is not None, rope=rope is not None, scale=scale,
                          f32_out=f32_out, heads=tn // HEAD_DIM),
        grid=(n_rows // tm, n_cols // tn), in_specs=in_specs, out_specs=out_specs, out_shape=out_shape,
        compiler_params=_cparams(("arbitrary", "arbitrary")), name=name,
    )(*args)


def _attn_kernel(*refs, diff, lam_init):
    if diff:
        lam_ref, subg_ref, q_ref, k_ref, v_ref, o_ref = refs
    else:
        q_ref, k_ref, v_ref, o_ref = refs
    outs = []
    for m in range(2):
        q = q_ref[0, :, m * HEAD_DIM:(m + 1) * HEAD_DIM]
        kk = k_ref[0, :, m * HEAD_DIM:(m + 1) * HEAD_DIM] if diff else k_ref[0]
        s = lax.dot_general(q, kk, (((1,), (1,)), ((), ())), preferred_element_type=F32)
        p = jnp.exp2(s - jnp.max(s, axis=-1, keepdims=True))
        l = jnp.sum(p, axis=-1, keepdims=True)
        outs.append(jnp.dot(p.astype(BF16), v_ref[0], preferred_element_type=F32) / l)
    if diff:
        lp = lam_ref[...]
        lam = (jnp.exp(jnp.sum(lp[0:1] * lp[1:2], axis=-1, keepdims=True))
               - jnp.exp(jnp.sum(lp[2:3] * lp[3:4], axis=-1, keepdims=True)) + lam_init)
        o = outs[0] - lam * outs[1]
        o = (o * lax.rsqrt(jnp.mean(o * o, axis=-1, keepdims=True) + EPS)) * subg_ref[...]
        o_ref[0] = (o * (1.0 - lam_init)).astype(BF16)
    else:
        o_ref[0, :, 0:HEAD_DIM] = outs[0].astype(BF16)
        o_ref[0, :, HEAD_DIM:2 * HEAD_DIM] = outs[1].astype(BF16)


def _attn_call(q, k, v, *, diff, lam_p=None, subln_g=None, lam_init=0.0, name="attn"):
    b, t, d = q.shape
    s = k.shape[1]
    groups = d // GROUP_LANES
    kw, vw = k.shape[2] // groups, v.shape[2] // groups
    tq = min(t, 256)
    assert t % tq == 0
    in_specs = [
        pl.BlockSpec((1, tq, GROUP_LANES), lambda bi, g, qi: (bi, qi, g)),
        pl.BlockSpec((1, s, kw), lambda bi, g, qi: (bi, 0, g)),
        pl.BlockSpec((1, s, vw), lambda bi, g, qi: (bi, 0, g)),
    ]
    args = [q, k, v]
    if diff:
        in_specs = [pl.BlockSpec((4, HEAD_DIM), lambda bi, g, qi: (0, 0)),
                    pl.BlockSpec((1, GROUP_LANES), lambda bi, g, qi: (0, 0))] + in_specs
        args = [lam_p, subln_g.reshape(1, GROUP_LANES)] + args
    return pl.pallas_call(
        functools.partial(_attn_kernel, diff=diff, lam_init=lam_init),
        grid=(b, groups, t // tq), in_specs=in_specs,
        out_specs=pl.BlockSpec((1, tq, GROUP_LANES), lambda bi, g, qi: (bi, qi, g)),
        out_shape=jax.ShapeDtypeStruct((b, t, d), BF16),
        compiler_params=_cparams(("arbitrary", "arbitrary", "arbitrary")), name=name,
    )(*args)


def _oproj_kernel(o_ref, w_ref, g_ref, x_ref, out_ref):
    y = jnp.dot(o_ref[...], w_ref[...], preferred_element_type=F32)
    out_ref[...] = x_ref[...] + g_ref[0] * y


def _oproj_call(o, w, mod48, x, *, row_off, tm, n_prompt, dec_seq):
    n_rows, d_in = o.shape
    d = x.shape[1]
    tn = 512 if d % 512 == 0 else d
    mo = row_off // tm
    row = functools.partial(_mod_row, tm=tm, n_prompt=n_prompt, dec_seq=dec_seq)
    return pl.pallas_call(
        _oproj_kernel,
        grid=(n_rows // tm, d // tn),
        in_specs=[
            pl.BlockSpec((tm, d_in), lambda m, n: (m, 0)),
            pl.BlockSpec((d_in, tn), lambda m, n: (0, n)),
            pl.BlockSpec((1, 1, tn), lambda m, n: (row(m + mo) * N_MOD + 2, 0, n)),
            pl.BlockSpec((tm, tn), lambda m, n: (m + mo, n)),
        ],
        out_specs=pl.BlockSpec((tm, tn), lambda m, n: (m + mo, n)),
        out_shape=jax.ShapeDtypeStruct(x.shape, F32),
        input_output_aliases={3: 0},
        compiler_params=_cparams(("arbitrary", "arbitrary")), name="oproj_resid",
    )(o, w, mod48, x)


def _moe_block(n_slots, n_exp):
    bm = 512
    while bm > 128 and n_slots // n_exp < 2 * bm:
        bm //= 2
    return bm


def _route_meta(topi, n_exp, bm):
    k, n = topi.shape
    nk = n * k
    e_flat = topi.T.reshape(nk)
    onehot = (e_flat[:, None] == jnp.arange(n_exp, dtype=I32)[None, :]).astype(I32)
    csum = jnp.cumsum(onehot, axis=0)
    rank = jnp.take_along_axis(csum, e_flat[:, None], axis=1)[:, 0] - 1
    counts = csum[-1]
    padded = (counts + bm - 1) // bm * bm
    pends = jnp.cumsum(padded)
    pstarts = pends - padded
    pos = pstarts[e_flat] + rank
    p_total = -(-nk // bm) * bm + n_exp * bm
    slot_tok = jnp.zeros((p_total,), I32).at[pos].set(jnp.arange(nk, dtype=I32) // k)
    nb = p_total // bm
    n_used = pends[-1] // bm
    blk = jnp.arange(nb, dtype=I32)
    be = jnp.minimum(jnp.searchsorted(pends, blk * bm, side="right").astype(I32), n_exp - 1)
    be = jnp.where(blk < n_used, be, be[jnp.maximum(n_used - 1, 0)])
    return slot_tok, pos.reshape(n, k).T, be, n_used.reshape(1).astype(I32)


def _row_gather(idx_ref, n_rows, src_hbm, dst, sem):
    def issue(r, c):
        pltpu.make_async_copy(src_hbm.at[pl.ds(idx_ref[0, 0, r], 1)], dst.at[pl.ds(r, 1)], sem).start()
        return c

    lax.fori_loop(0, n_rows, issue, 0, unroll=8)


def _row_gather_wait(n_rows, src_hbm, dst, sem):
    pltpu.make_async_copy(src_hbm.at[pl.ds(0, n_rows)], dst, sem).wait()


def _gather_kernel(nu_ref, idx_cur, idx_nxt, h_hbm, o_ref, buf, sem, *, rows, bm, steps):
    i = pl.program_id(0)
    slot = i % 2
    rows_used = nu_ref[0] * bm

    @pl.when((i == 0) & (rows_used > 0))
    def _():
        _row_gather(idx_cur, rows, h_hbm, buf.at[0], sem.at[0])

    @pl.when((i + 1 < steps) & ((i + 1) * rows < rows_used))
    def _():
        _row_gather(idx_nxt, rows, h_hbm, buf.at[1 - slot], sem.at[1 - slot])

    @pl.when(i * rows < rows_used)
    def _():
        _row_gather_wait(rows, h_hbm, buf.at[slot], sem.at[slot])
        o_ref[...] = buf[slot].astype(BF16)

    @pl.when(i * rows >= rows_used)
    def _():
        o_ref[...] = jnp.zeros_like(o_ref)


def _gather_call(h, slot_tok, n_used, bm):
    n, d = h.shape
    p_total = slot_tok.shape[0]
    rows = min(256, bm)
    steps = p_total // rows
    idx = slot_tok.reshape(steps, 1, rows)
    return pl.pallas_call(
        functools.partial(_gather_kernel, rows=rows, bm=bm, steps=steps),
        grid_spec=pltpu.PrefetchScalarGridSpec(
            num_scalar_prefetch=1, grid=(steps,),
            in_specs=[pl.BlockSpec((1, 1, rows), lambda i, nu: (i, 0, 0), memory_space=pltpu.SMEM),
                      pl.BlockSpec((1, 1, rows), lambda i, nu: (jnp.minimum(i + 1, steps - 1), 0, 0),
                                   memory_space=pltpu.SMEM),
                      pl.BlockSpec(memory_space=pl.ANY)],
            out_specs=pl.BlockSpec((rows, d), lambda i, nu: (i, 0)),
            scratch_shapes=[pltpu.VMEM((2, rows, d), F32), pltpu.SemaphoreType.DMA((2,))],
        ),
        out_shape=jax.ShapeDtypeStruct((p_total, d), BF16),
        compiler_params=_cparams(("arbitrary",)), name="moe_gather",
    )(n_used, idx, idx, h)


def _expert_changes(be_ref, b):
    return (b == 0) | (be_ref[b] != be_ref[jnp.maximum(b - 1, 0)])


W_PREP_ROWS = 256


def _gmm1_kernel(be_ref, nu_ref, x_ref, w_ref, bg_ref, bl_ref, o_ref, wg_scr, wl_scr):
    b = pl.program_id(1)
    used = b < nu_ref[0]
    kdim = w_ref.shape[1]
    tf = wg_scr.shape[1]

    @pl.when(used & _expert_changes(be_ref, b))
    def _():
        lane = lax.broadcasted_iota(I32, (W_PREP_ROWS, LANES), 1)
        even = (2 * lane) % LANES
        odd = even + 1
        low = lane < LANES // 2

        def rows(r, c):
            r0 = pl.multiple_of(r * W_PREP_ROWS, W_PREP_ROWS)
            for j in range(tf // LANES):
                a = w_ref[0, pl.ds(r0, W_PREP_ROWS), 2 * j * LANES:(2 * j + 1) * LANES]
                z = w_ref[0, pl.ds(r0, W_PREP_ROWS), (2 * j + 1) * LANES:(2 * j + 2) * LANES]
                g = jnp.where(low, jnp.take_along_axis(a, even, axis=1), jnp.take_along_axis(z, even, axis=1))
                u = jnp.where(low, jnp.take_along_axis(a, odd, axis=1), jnp.take_along_axis(z, odd, axis=1))
                wg_scr[pl.ds(r0, W_PREP_ROWS), j * LANES:(j + 1) * LANES] = g.astype(BF16)
                wl_scr[pl.ds(r0, W_PREP_ROWS), j * LANES:(j + 1) * LANES] = u.astype(BF16)
            return c

        lax.fori_loop(0, kdim // W_PREP_ROWS, rows, 0)

    @pl.when(used)
    def _():
        x = x_ref[...]
        g = jnp.dot(x, wg_scr[...], preferred_element_type=F32) + bg_ref[0]
        u = jnp.dot(x, wl_scr[...], preferred_element_type=F32) + bl_ref[0]
        g = jnp.minimum(g, SWIGLU_LIMIT)
        u = jnp.clip(u, -SWIGLU_LIMIT, SWIGLU_LIMIT)
        o_ref[...] = (g * jax.nn.sigmoid(SWIGLU_ALPHA * g) * (u + 1.0)).astype(BF16)

    @pl.when(jnp.logical_not(used))
    def _():
        o_ref[...] = jnp.zeros_like(o_ref)


def _gmm1_call(xs, w_gu, bg, bl, be, n_used, bm):
    p_total, d = xs.shape
    n_exp, _, f2 = w_gu.shape
    f = f2 // 2
    tf = 512 if f % 512 == 0 else f
    nb = p_total // bm
    bspec = pl.BlockSpec((1, 1, tf), lambda n, b, be, nu: (be[b], 0, n))
    return pl.pallas_call(
        _gmm1_kernel,
        grid_spec=pltpu.PrefetchScalarGridSpec(
            num_scalar_prefetch=2, grid=(f // tf, nb),
            in_specs=[pl.BlockSpec((bm, d), lambda n, b, be, nu: (b, 0)),
                      pl.BlockSpec((1, d, 2 * tf), lambda n, b, be, nu: (be[b], 0, n)),
                      bspec, bspec],
            out_specs=pl.BlockSpec((bm, tf), lambda n, b, be, nu: (b, n)),
            scratch_shapes=[pltpu.VMEM((d, tf), BF16), pltpu.VMEM((d, tf), BF16)],
        ),
        out_shape=jax.ShapeDtypeStruct((p_total, f), BF16),
        compiler_params=_cparams(("arbitrary", "arbitrary")), name="moe_gate_up",
    )(be, n_used, xs, w_gu, bg, bl)


def _gmm2_kernel(be_ref, nu_ref, a_ref, w_ref, b_ref, o_ref, w_scr):
    b = pl.program_id(1)
    used = b < nu_ref[0]
    kdim = w_ref.shape[1]

    @pl.when(used & _expert_changes(be_ref, b))
    def _():
        def rows(r, c):
            r0 = pl.multiple_of(r * W_PREP_ROWS, W_PREP_ROWS)
            w_scr[pl.ds(r0, W_PREP_ROWS), :] = w_ref[0, pl.ds(r0, W_PREP_ROWS), :].astype(BF16)
            return c

        lax.fori_loop(0, kdim // W_PREP_ROWS, rows, 0)

    @pl.when(used)
    def _():
        o_ref[...] = jnp.dot(a_ref[...], w_scr[...], preferred_element_type=F32) + b_ref[0]

    @pl.when(jnp.logical_not(used))
    def _():
        o_ref[...] = jnp.zeros_like(o_ref)


def _gmm2_call(act, wd, bd, be, n_used, bm):
    p_total, f = act.shape
    n_exp, _, d = wd.shape
    tn = 1024 if d % 1024 == 0 else d
    nb = p_total // bm
    return pl.pallas_call(
        _gmm2_kernel,
        grid_spec=pltpu.PrefetchScalarGridSpec(
            num_scalar_prefetch=2, grid=(d // tn, nb),
            in_specs=[pl.BlockSpec((bm, f), lambda n, b, be, nu: (b, 0)),
                      pl.BlockSpec((1, f, tn), lambda n, b, be, nu: (be[b], 0, n)),
                      pl.BlockSpec((1, 1, tn), lambda n, b, be, nu: (be[b], 0, n))],
            out_specs=pl.BlockSpec((bm, tn), lambda n, b, be, nu: (b, n)),
            scratch_shapes=[pltpu.VMEM((f, tn), BF16)],
        ),
        out_shape=jax.ShapeDtypeStruct((p_total, d), F32),
        compiler_params=_cparams(("arbitrary", "arbitrary")), name="moe_down",
    )(be, n_used, act, wd, bd)


COMBINE_LANE_CHUNK = 512


def _combine_kernel(pos_cur, pos_nxt, gate_ref, g2_ref, x_ref, y_hbm, out_ref, buf, sem, *, tc, steps):
    i = pl.program_id(0)
    slot = i % 2
    n_slots = TOP_K * tc

    @pl.when(i == 0)
    def _():
        _row_gather(pos_cur, n_slots, y_hbm, buf.at[0], sem.at[0])

    @pl.when(i + 1 < steps)
    def _():
        _row_gather(pos_nxt, n_slots, y_hbm, buf.at[1 - slot], sem.at[1 - slot])

    _row_gather_wait(n_slots, y_hbm, buf.at[slot], sem.at[slot])
    gates = gate_ref[...]
    g2 = g2_ref[0]
    d = x_ref.shape[1]
    for c0 in range(0, d, COMBINE_LANE_CHUNK):
        seg = slice(c0, min(c0 + COMBINE_LANE_CHUNK, d))
        acc = None
        for k in range(TOP_K):
            term = gates[:, k:k + 1] * buf[slot, k * tc:(k + 1) * tc, seg]
            acc = term if acc is None else acc + term
        out_ref[:, seg] = x_ref[:, seg] + g2[:, seg] * acc


def _combine_call(x, y, pos, gates, mod48, *, n_prompt, dec_seq):
    n, d = x.shape
    tc = 128
    steps = n // tc
    row = functools.partial(_mod_row, tm=tc, n_prompt=n_prompt, dec_seq=dec_seq)
    idx = pos.reshape(TOP_K, steps, tc).transpose(1, 0, 2).reshape(steps, 1, TOP_K * tc)
    return pl.pallas_call(
        functools.partial(_combine_kernel, tc=tc, steps=steps),
        grid=(steps,),
        in_specs=[
            pl.BlockSpec((1, 1, TOP_K * tc), lambda i: (i, 0, 0), memory_space=pltpu.SMEM),
            pl.BlockSpec((1, 1, TOP_K * tc), lambda i: (jnp.minimum(i + 1, steps - 1), 0, 0),
                         memory_space=pltpu.SMEM),
            pl.BlockSpec((tc, TOP_K), lambda i: (i, 0)),
            pl.BlockSpec((1, 1, d), lambda i: (row(i) * N_MOD + 5, 0, 0)),
            pl.BlockSpec((tc, d), lambda i: (i, 0)),
            pl.BlockSpec(memory_space=pl.ANY),
        ],
        out_specs=pl.BlockSpec((tc, d), lambda i: (i, 0)),
        out_shape=jax.ShapeDtypeStruct(x.shape, F32),
        scratch_shapes=[pltpu.VMEM((2, TOP_K * tc, d), F32), pltpu.SemaphoreType.DMA((2,))],
        input_output_aliases={4: 0},
        compiler_params=_cparams(("arbitrary",)), name="moe_combine",
    )(idx, idx, gates, mod48, x, y)


def _final_kernel(x_ref, g_ref, o_ref):
    x = x_ref[...]
    o_ref[...] = (x * lax.rsqrt(jnp.mean(x * x, axis=-1, keepdims=True) + EPS)) * g_ref[...]


def _final_call(x, g, *, row_off, n_rows, tm):
    d = x.shape[1]
    mo = row_off // tm
    return pl.pallas_call(
        _final_kernel, grid=(n_rows // tm,),
        in_specs=[pl.BlockSpec((tm, d), lambda m: (m + mo, 0)), pl.BlockSpec((1, d), lambda m: (0, 0))],
        out_specs=pl.BlockSpec((tm, d), lambda m: (m, 0)),
        out_shape=jax.ShapeDtypeStruct((n_rows, d), F32),
        compiler_params=_cparams(("arbitrary",)), name="final_norm",
    )(x, g.reshape(1, d))


def _rope_tables(t):
    rows = t // GRID_W
    row = jnp.repeat(jnp.arange(rows), GRID_W).astype(F32)
    col = jnp.tile(jnp.arange(GRID_W), rows).astype(F32)
    nf = HEAD_DIM // 4
    inv = ROPE_THETA ** (-jnp.arange(nf, dtype=F32) / nf)
    ar = row[:, None] * inv
    ac = col[:, None] * inv
    ang = jnp.concatenate([ar, ar, ac, ac], axis=-1)
    sign = jnp.concatenate([-jnp.ones((nf,), F32), jnp.ones((nf,), F32)] * 2)
    return jnp.cos(ang), jnp.sin(ang) * sign


def _lambda_init(i):
    return 0.8 - 0.6 * math.exp(-0.3 * i)


def kernel(x_prompt, x_sample, cache_k_diff, cache_v_diff, cache_k_gqa, cache_v_gqa, c, c_ctx, w_ada, b_ada, norm_mix_g, norm_ffn_g, diff_w_qkv, diff_w_o, diff_lambda, diff_subln_g, gqa_w_qkv, gqa_w_o, gqa_q_norm_g, gqa_k_norm_g, moe_w_router, moe_b_router, moe_w_gate_up, moe_b_gate_up, moe_w_down, moe_b_down, final_norm_g):
    bp, tp, d = x_prompt.shape
    bs, ts, _ = x_sample.shape
    depth = w_ada.shape[0]
    n_exp = moe_w_router.shape[-1]
    d_ff = moe_w_down.shape[2]
    past = cache_k_diff.shape[2]
    n_prompt, n_sample = bp * tp, bs * ts
    n = n_prompt + n_sample
    tm = _tiles(n_prompt, ts)
    w_diff = d
    wq, wk = d, d // 2
    qscale = HEAD_DIM ** -0.5 * LOG2E

    x = jnp.concatenate([x_prompt.reshape(n_prompt, d), x_sample.reshape(n_sample, d)], axis=0)
    cond8 = jnp.zeros((8, d), F32).at[0].set(c_ctx).at[1:1 + bs].set(c)
    mod = _ada_call(cond8, w_ada, b_ada)
    rope = _rope_tables(ts)
    bm = _moe_block(n * TOP_K, n_exp)

    kd, vd, kg, vg = [], [], [], []
    for i in range(depth):
        mod48 = mod[i].reshape(8 * N_MOD, 1, d)
        h = _normmod_call(x, norm_mix_g[i], mod48, 1, 0, n_prompt, ts)
        j = i // 2
        if i % 2 == 0:
            w = diff_w_qkv[j].astype(BF16)
            pk = dict(row_off=0, n_rows=n_prompt, n_cols=w_diff, tm=tm)
            sk = dict(row_off=n_prompt, n_rows=n_sample, n_cols=w_diff, tm=tm)
            (q_p,) = _proj_call(h, w, col_off=0, scale=qscale, name="diff_q_p", **pk)
            k_p, k32 = _proj_call(h, w, col_off=w_diff, f32_out=True, name="diff_k_p", **pk)
            v_p, v32 = _proj_call(h, w, col_off=2 * w_diff, f32_out=True, name="diff_v_p", **pk)
            (q_s,) = _proj_call(h, w, col_off=0, rope=rope, scale=qscale, name="diff_q_s", **sk)
            (k_s,) = _proj_call(h, w, col_off=w_diff, rope=rope, name="diff_k_s", **sk)
            (v_s,) = _proj_call(h, w, col_off=2 * w_diff, name="diff_v_s", **sk)
            kd.append(k32.reshape(bp, tp, d // GROUP_LANES, GROUP_LANES))
            vd.append(v32.reshape(bp, tp, d // GROUP_LANES, GROUP_LANES))
            ck = cache_k_diff[:, j].reshape(bs, past, w_diff).astype(BF16)
            cv = cache_v_diff[:, j].reshape(bs, past, w_diff).astype(BF16)
            ak = dict(diff=True, lam_p=diff_lambda[j], subln_g=diff_subln_g[j], lam_init=_lambda_init(i))
            kc, vc = w_diff, w_diff
            w_o = diff_w_o[j]
        else:
            w = gqa_w_qkv[j].astype(BF16)
            pk = dict(row_off=0, n_rows=n_prompt, tm=tm)
            sk = dict(row_off=n_prompt, n_rows=n_sample, tm=tm)
            qg, kgain = gqa_q_norm_g[j], gqa_k_norm_g[j]
            (q_p,) = _proj_call(h, w, col_off=0, n_cols=wq, gain=qg, scale=qscale, name="gqa_q_p", **pk)
            k_p, k32 = _proj_call(h, w, col_off=wq, n_cols=wk, gain=kgain, f32_out=True, name="gqa_k_p", **pk)
            v_p, v32 = _proj_call(h, w, col_off=wq + wk, n_cols=wk, f32_out=True, name="gqa_v_p", **pk)
            (q_s,) = _proj_call(h, w, col_off=0, n_cols=wq, gain=qg, rope=rope, scale=qscale, name="gqa_q_s", **sk)
            (k_s,) = _proj_call(h, w, col_off=wq, n_cols=wk, gain=kgain, rope=rope, name="gqa_k_s", **sk)
            (v_s,) = _proj_call(h, w, col_off=wq + wk, n_cols=wk, name="gqa_v_s", **sk)
            kg.append(k32.reshape(bp, tp, wk // HEAD_DIM, HEAD_DIM))
            vg.append(v32.reshape(bp, tp, wk // HEAD_DIM, HEAD_DIM))
            ck = cache_k_gqa[:, j].reshape(bs, past, wk).astype(BF16)
            cv = cache_v_gqa[:, j].reshape(bs, past, wk).astype(BF16)
            ak = dict(diff=False)
            kc, vc = wk, wk
            w_o = gqa_w_o[j]
        o_p = _attn_call(q_p.reshape(bp, tp, d), k_p.reshape(bp, tp, kc), v_p.reshape(bp, tp, vc),
                         name="attn_p", **ak)
        k_all = jnp.concatenate([ck, k_s.reshape(bs, ts, kc)], axis=1)
        v_all = jnp.concatenate([cv, v_s.reshape(bs, ts, vc)], axis=1)
        o_s = _attn_call(q_s.reshape(bs, ts, d), k_all, v_all, name="attn_s", **ak)
        w_o = w_o.astype(BF16)
        ok = dict(tm=tm, n_prompt=n_prompt, dec_seq=ts)
        x = _oproj_call(o_p.reshape(n_prompt, d), w_o, mod48, x, row_off=0, **ok)
        x = _oproj_call(o_s.reshape(n_sample, d), w_o, mod48, x, row_off=n_prompt, **ok)

        wr = jnp.zeros((d, LANES), F32).at[:, :n_exp].set(moe_w_router[i])
        wr_hi = wr.astype(BF16)
        wr_lo = (wr - wr_hi.astype(F32)).astype(BF16)
        br = jnp.zeros((1, LANES), F32).at[0, :n_exp].set(moe_b_router[i])
        h, topi, gates = _normmod_call(x, norm_ffn_g[i], mod48, 4, 3, n_prompt, ts,
                                       router=(wr_hi, wr_lo, br, n_exp))
        slot_tok, pos, be, n_used = _route_meta(topi, n_exp, bm)
        xs = _gather_call(h, slot_tok, n_used, bm)
        b_gu = moe_b_gate_up[i]
        act = _gmm1_call(xs, moe_w_gate_up[i], b_gu[:, None, 0::2], b_gu[:, None, 1::2], be, n_used, bm)
        yb = _gmm2_call(act, moe_w_down[i], moe_b_down[i][:, None, :], be, n_used, bm)
        x = _combine_call(x, yb, pos, gates.T, mod48, n_prompt=n_prompt, dec_seq=ts)

    y_p = _final_call(x, final_norm_g, row_off=0, n_rows=n_prompt, tm=tm)
    y_s = _final_call(x, final_norm_g, row_off=n_prompt, n_rows=n_sample, tm=tm)
    return (y_p.reshape(bp, tp, d), y_s.reshape(bs, ts, d),
            jnp.stack(kd, axis=1), jnp.stack(vd, axis=1), jnp.stack(kg, axis=1), jnp.stack(vg, axis=1))
```

```python
import functools
import math

import jax
import jax.numpy as jnp
from jax import lax
from jax.experimental import pallas as pl
from jax.experimental.pallas import tpu as pltpu

F32 = jnp.float32
BF16 = jnp.bfloat16
I32 = jnp.int32

HEAD_DIM = 128
GROUP_LANES = 2 * HEAD_DIM
GRID_W = 64
ROPE_THETA = 10000.0
TOP_K = 4
SWIGLU_ALPHA = 1.702
SWIGLU_LIMIT = 7.0
EPS = 1e-6
N_MOD = 6
LOG2E = 1.4426950408889634

VMEM_LIMIT_BYTES = 56 * 1024 * 1024
LANES = 128
NEG_BIG = -1e30


def _cparams(sem):
    return pltpu.CompilerParams(dimension_semantics=sem, vmem_limit_bytes=VMEM_LIMIT_BYTES)


def _tiles(n_prompt, dec_seq):
    tm = 512
    while n_prompt % tm or dec_seq % tm:
        tm //= 2
    return tm


def _mod_row(m, tm, n_prompt, dec_seq):
    mp = n_prompt // tm
    return jnp.where(m < mp, 0, 1 + (m - mp) // (dec_seq // tm))


def _ada_kernel(c_ref, w_ref, b_ref, o_ref):
    c = c_ref[...]
    a = (c * jax.nn.sigmoid(c)).astype(BF16)
    w = w_ref[0].astype(BF16)
    o_ref[0] = jnp.dot(a, w, preferred_element_type=F32) + b_ref[0]


def _ada_call(cond8, w_ada, b_ada):
    depth, d, nmod = w_ada.shape
    tn = 1024 if nmod % 1024 == 0 else nmod
    return pl.pallas_call(
        _ada_kernel,
        grid=(depth, nmod // tn),
        in_specs=[
            pl.BlockSpec((8, d), lambda l, n: (0, 0)),
            pl.BlockSpec((1, d, tn), lambda l, n: (l, 0, n)),
            pl.BlockSpec((1, 1, tn), lambda l, n: (l, 0, n)),
        ],
        out_specs=pl.BlockSpec((1, 8, tn), lambda l, n: (l, 0, n)),
        out_shape=jax.ShapeDtypeStruct((depth, 8, nmod), F32),
        compiler_params=_cparams(("arbitrary", "arbitrary")),
        name="ada_mod",
    )(cond8, w_ada, b_ada.reshape(depth, 1, nmod))


def _normmod(x, g, sc, sh):
    y = x * lax.rsqrt(jnp.mean(x * x, axis=-1, keepdims=True) + EPS)
    return (y * g) * (1.0 + sc) + sh


def _normmod_kernel(x_ref, g_ref, sc_ref, sh_ref, h_ref):
    h_ref[...] = _normmod(x_ref[...], g_ref[...], sc_ref[0], sh_ref[0]).astype(BF16)


def _normmod_router_kernel(x_ref, g_ref, sc_ref, sh_ref, wh_ref, wl_ref, br_ref,
                           h_ref, ti_ref, gt_ref, *, n_exp):
    h = _normmod(x_ref[...], g_ref[...], sc_ref[0], sh_ref[0])
    h_ref[...] = h
    hi = h.astype(BF16)
    lo = (h - hi.astype(F32)).astype(BF16)
    wh = wh_ref[...]
    lg = (jnp.dot(hi, wh, preferred_element_type=F32)
          + jnp.dot(lo, wh, preferred_element_type=F32)
          + jnp.dot(hi, wl_ref[...], preferred_element_type=F32)) + br_ref[...]
    cur = lg.T[:n_exp]
    eid = lax.broadcasted_iota(I32, cur.shape, 0)
    vals, idxs = [], []
    for _ in range(TOP_K):
        mx = jnp.max(cur, axis=0, keepdims=True)
        ix = jnp.min(jnp.where(cur == mx, eid, n_exp), axis=0, keepdims=True)
        vals.append(mx)
        idxs.append(ix)
        cur = jnp.where(eid == ix, jnp.finfo(F32).min, cur)
    ex = [jnp.exp(v - vals[0]) for v in vals]
    den = ex[0] + ex[1] + ex[2] + ex[3]
    for k in range(TOP_K):
        ti_ref[k:k + 1, :] = idxs[k]
        gt_ref[k:k + 1, :] = ex[k] / den


def _normmod_call(x, g, mod48, chunk_sc, chunk_sh, n_prompt, dec_seq, router=None):
    n, d = x.shape
    tm = _tiles(n_prompt, dec_seq)
    row = functools.partial(_mod_row, tm=tm, n_prompt=n_prompt, dec_seq=dec_seq)
    in_specs = [
        pl.BlockSpec((tm, d), lambda m: (m, 0)),
        pl.BlockSpec((1, d), lambda m: (0, 0)),
        pl.BlockSpec((1, 1, d), lambda m: (row(m) * N_MOD + chunk_sc, 0, 0)),
        pl.BlockSpec((1, 1, d), lambda m: (row(m) * N_MOD + chunk_sh, 0, 0)),
    ]
    h_spec = pl.BlockSpec((tm, d), lambda m: (m, 0))
    h_shape = jax.ShapeDtypeStruct((n, d), BF16)
    if router is None:
        return pl.pallas_call(
            _normmod_kernel, grid=(n // tm,), in_specs=in_specs, out_specs=h_spec, out_shape=h_shape,
            compiler_params=_cparams(("arbitrary",)), name="normmod",
        )(x, g.reshape(1, d), mod48, mod48)
    wr_hi, wr_lo, br, n_exp = router
    in_specs += [
        pl.BlockSpec((d, LANES), lambda m: (0, 0)),
        pl.BlockSpec((d, LANES), lambda m: (0, 0)),
        pl.BlockSpec((1, LANES), lambda m: (0, 0)),
    ]
    kv_spec = pl.BlockSpec((TOP_K, tm), lambda m: (0, m))
    return pl.pallas_call(
        functools.partial(_normmod_router_kernel, n_exp=n_exp),
        grid=(n // tm,), in_specs=in_specs,
        out_specs=[h_spec, kv_spec, kv_spec],
        out_shape=[jax.ShapeDtypeStruct((n, d), F32), jax.ShapeDtypeStruct((TOP_K, n), I32),
                   jax.ShapeDtypeStruct((TOP_K, n), F32)],
        compiler_params=_cparams(("arbitrary",)), name="normmod_router",
    )(x, g.reshape(1, d), mod48, mod48, wr_hi, wr_lo, br)


def _proj_kernel(*refs, norm, rope, scale, f32_out, heads):
    it = iter(refs)
    x_ref, w_ref = next(it), next(it)
    gain_ref = next(it) if norm else None
    cos_ref, sin_ref = (next(it), next(it)) if rope else (None, None)
    o_ref = next(it)
    o32_ref = next(it) if f32_out else None
    x = x_ref[...]
    if rope:
        cos, sin = cos_ref[...], sin_ref[...]
        lane = lax.broadcasted_iota(I32, cos.shape, 1)
        first_half = (lane % (HEAD_DIM // 2)) < (HEAD_DIM // 4)
    for j in range(heads):
        seg = slice(j * HEAD_DIM, (j + 1) * HEAD_DIM)
        if j % 2 == 0:
            pair = slice(j * HEAD_DIM, min(j + 2, heads) * HEAD_DIM)
            acc = jnp.dot(x, w_ref[:, pair], preferred_element_type=F32)
        y = acc[:, (j % 2) * HEAD_DIM:(j % 2 + 1) * HEAD_DIM]
        if norm:
            y = (y * lax.rsqrt(jnp.mean(y * y, axis=-1, keepdims=True) + EPS)) * gain_ref[...]
        if f32_out:
            o32_ref[:, seg] = y
        if rope:
            r = jnp.where(first_half, pltpu.roll(y, HEAD_DIM - HEAD_DIM // 4, 1), pltpu.roll(y, HEAD_DIM // 4, 1))
            y = y * cos + r * sin
        if scale != 1.0:
            y = y * scale
        o_ref[:, seg] = y.astype(BF16)


def _proj_call(h, w, *, row_off, n_rows, col_off, n_cols, tm, gain=None, rope=None, scale=1.0,
               f32_out=False, name="proj"):
    d = h.shape[1]
    tn = 512 if n_cols % 512 == 0 else n_cols
    mo, no = row_off // tm, col_off // tn
    in_specs = [
        pl.BlockSpec((tm, d), lambda m, n: (m + mo, 0)),
        pl.BlockSpec((d, tn), lambda m, n: (0, n + no)),
    ]
    args = [h, w]
    if gain is not None:
        in_specs.append(pl.BlockSpec((1, HEAD_DIM), lambda m, n: (0, 0)))
        args.append(gain.reshape(1, HEAD_DIM))
    if rope is not None:
        cos, sin = rope
        tb = cos.shape[0] // tm
        in_specs += [pl.BlockSpec((tm, HEAD_DIM), lambda m, n: (m % tb, 0))] * 2
        args += [cos, sin]
    out_specs = [pl.BlockSpec((tm, tn), lambda m, n: (m, n))]
    out_shape = [jax.ShapeDtypeStruct((n_rows, n_cols), BF16)]
    if f32_out:
        out_specs.append(pl.BlockSpec((tm, tn), lambda m, n: (m, n)))
        out_shape.append(jax.ShapeDtypeStruct((n_rows, n_cols), F32))
    return pl.pallas_call(
        functools.partial(_proj_kernel, norm=gain is not None, rope=rope is not None, scale=scale,
                          f32_out=f32_out, heads=tn // HEAD_DIM),
        grid=(n_rows // tm, n_cols // tn), in_specs=in_specs, out_specs=out_specs, out_shape=out_shape,
        compiler_params=_cparams(("arbitrary", "arbitrary")), name=name,
    )(*args)


def _attn_kernel(*refs, diff, ctx, lam_init):
    it = iter(refs)
    lam_ref, subg_ref = (next(it), next(it)) if diff else (None, None)
    q_ref, k_ref, v_ref = next(it), next(it), next(it)
    ck_ref, cv_ref = (next(it), next(it)) if ctx else (None, None)
    o_ref = next(it)
    nt = (((1,), (1,)), ((), ()))
    ss = []
    for m in range(2):
        seg = slice(m * HEAD_DIM, (m + 1) * HEAD_DIM)
        q = q_ref[0, :, seg]
        s = [lax.dot_general(q, k_ref[0, :, seg] if diff else k_ref[0], nt, preferred_element_type=F32)]
        if ctx:
            ck = (ck_ref[0, 0, :, seg] if diff else ck_ref[0, 0]).astype(BF16)
            s.append(lax.dot_general(q, ck, nt, preferred_element_type=F32))
        ss.append(s)
    ps, ls = [], []
    for m in range(2):
        mx = functools.reduce(jnp.maximum, [jnp.max(s, axis=-1, keepdims=True) for s in ss[m]])
        p = [jnp.exp2(s - mx) for s in ss[m]]
        ls.append(functools.reduce(jnp.add, [jnp.sum(pp, axis=-1, keepdims=True) for pp in p]))
        ps.append([pp.astype(BF16) for pp in p])
    outs = []
    for m in range(2):
        acc = jnp.dot(ps[m][0], v_ref[0], preferred_element_type=F32)
        if ctx:
            acc = acc + jnp.dot(ps[m][1], cv_ref[0, 0].astype(BF16), preferred_element_type=F32)
        outs.append(acc / ls[m])
    if diff:
        lp = lam_ref[...]
        lam = (jnp.exp(jnp.sum(lp[0:1] * lp[1:2], axis=-1, keepdims=True))
               - jnp.exp(jnp.sum(lp[2:3] * lp[3:4], axis=-1, keepdims=True)) + lam_init)
        o = outs[0] - lam * outs[1]
        o = (o * lax.rsqrt(jnp.mean(o * o, axis=-1, keepdims=True) + EPS)) * subg_ref[...]
        o_ref[0] = (o * (1.0 - lam_init)).astype(BF16)
    else:
        o_ref[0, :, 0:HEAD_DIM] = outs[0].astype(BF16)
        o_ref[0, :, HEAD_DIM:2 * HEAD_DIM] = outs[1].astype(BF16)


def _attn_call(q, k, v, *, diff, ctx=None, lam_p=None, subln_g=None, lam_init=0.0, name="attn"):
    b, t, d = q.shape
    s = k.shape[1]
    groups = d // GROUP_LANES
    kw, vw = k.shape[2] // groups, v.shape[2] // groups
    tq = min(t, 256)
    assert t % tq == 0
    in_specs = [
        pl.BlockSpec((1, tq, GROUP_LANES), lambda bi, g, qi: (bi, qi, g)),
        pl.BlockSpec((1, s, kw), lambda bi, g, qi: (bi, 0, g)),
        pl.BlockSpec((1, s, vw), lambda bi, g, qi: (bi, 0, g)),
    ]
    args = [q, k, v]
    if ctx is not None:
        ck, cv, layer = ctx
        past = ck.shape[2]
        in_specs += [pl.BlockSpec((1, 1, past, kw), lambda bi, g, qi: (bi, layer, 0, g)),
                     pl.BlockSpec((1, 1, past, vw), lambda bi, g, qi: (bi, layer, 0, g))]
        args += [ck, cv]
    if diff:
        in_specs = [pl.BlockSpec((4, HEAD_DIM), lambda bi, g, qi: (0, 0)),
                    pl.BlockSpec((1, GROUP_LANES), lambda bi, g, qi: (0, 0))] + in_specs
        args = [lam_p, subln_g.reshape(1, GROUP_LANES)] + args
    return pl.pallas_call(
        functools.partial(_attn_kernel, diff=diff, ctx=ctx is not None, lam_init=lam_init),
        grid=(b, groups, t // tq), in_specs=in_specs,
        out_specs=pl.BlockSpec((1, tq, GROUP_LANES), lambda bi, g, qi: (bi, qi, g)),
        out_shape=jax.ShapeDtypeStruct((b, t, d), BF16),
        compiler_params=_cparams(("arbitrary", "arbitrary", "arbitrary")), name=name,
    )(*args)


def _oproj_kernel(o_ref, w_ref, g_ref, x_ref, out_ref):
    y = jnp.dot(o_ref[...], w_ref[...], preferred_element_type=F32)
    out_ref[...] = x_ref[...] + g_ref[0] * y


def _oproj_call(o, w, mod48, x, *, row_off, tm, n_prompt, dec_seq):
    n_rows, d_in = o.shape
    d = x.shape[1]
    tn = 512 if d % 512 == 0 else d
    mo = row_off // tm
    row = functools.partial(_mod_row, tm=tm, n_prompt=n_prompt, dec_seq=dec_seq)
    return pl.pallas_call(
        _oproj_kernel,
        grid=(n_rows // tm, d // tn),
        in_specs=[
            pl.BlockSpec((tm, d_in), lambda m, n: (m, 0)),
            pl.BlockSpec((d_in, tn), lambda m, n: (0, n)),
            pl.BlockSpec((1, 1, tn), lambda m, n: (row(m + mo) * N_MOD + 2, 0, n)),
            pl.BlockSpec((tm, tn), lambda m, n: (m + mo, n)),
        ],
        out_specs=pl.BlockSpec((tm, tn), lambda m, n: (m + mo, n)),
        out_shape=jax.ShapeDtypeStruct(x.shape, F32),
        input_output_aliases={3: 0},
        compiler_params=_cparams(("arbitrary", "arbitrary")), name="oproj_resid",
    )(o, w, mod48, x)


def _moe_block(n_slots, n_exp):
    bm = 512
    while bm > 128 and n_slots // n_exp < 2 * bm:
        bm //= 2
    return bm


def _route_meta(topi, n_exp, bm):
    k, n = topi.shape
    nk = n * k
    e_flat = topi.T.reshape(nk)
    onehot = (e_flat[:, None] == jnp.arange(n_exp, dtype=I32)[None, :]).astype(I32)
    csum = jnp.cumsum(onehot, axis=0)
    rank = jnp.take_along_axis(csum, e_flat[:, None], axis=1)[:, 0] - 1
    counts = csum[-1]
    padded = (counts + bm - 1) // bm * bm
    pends = jnp.cumsum(padded)
    pstarts = pends - padded
    pos = pstarts[e_flat] + rank
    p_total = -(-nk // bm) * bm + n_exp * bm
    slot_tok = jnp.zeros((p_total,), I32).at[pos].set(jnp.arange(nk, dtype=I32) // k)
    nb = p_total // bm
    n_used = pends[-1] // bm
    blk = jnp.arange(nb, dtype=I32)
    be = jnp.minimum(jnp.searchsorted(pends, blk * bm, side="right").astype(I32), n_exp - 1)
    be = jnp.where(blk < n_used, be, be[jnp.maximum(n_used - 1, 0)])
    return slot_tok, pos.reshape(n, k).T, be, n_used.reshape(1).astype(I32)


def _row_gather(idx_ref, n_rows, src_hbm, dst, sem):
    for r in range(n_rows):
        pltpu.make_async_copy(src_hbm.at[pl.ds(idx_ref[0, 0, r], 1)], dst.at[pl.ds(r, 1)], sem).start()


def _row_gather_wait(n_rows, src_hbm, dst, sem):
    pltpu.make_async_copy(src_hbm.at[pl.ds(0, n_rows)], dst, sem).wait()


def _gather_kernel(nu_ref, idx_cur, idx_nxt, h_hbm, o_ref, buf, sem, *, rows, bm, steps):
    i = pl.program_id(0)
    slot = i % 2
    rows_used = nu_ref[0] * bm

    @pl.when((i == 0) & (rows_used > 0))
    def _():
        _row_gather(idx_cur, rows, h_hbm, buf.at[0], sem.at[0])

    @pl.when((i + 1 < steps) & ((i + 1) * rows < rows_used))
    def _():
        _row_gather(idx_nxt, rows, h_hbm, buf.at[1 - slot], sem.at[1 - slot])

    @pl.when(i * rows < rows_used)
    def _():
        _row_gather_wait(rows, h_hbm, buf.at[slot], sem.at[slot])
        o_ref[...] = buf[slot].astype(BF16)

    @pl.when(i * rows >= rows_used)
    def _():
        o_ref[...] = jnp.zeros_like(o_ref)


def _gather_call(h, slot_tok, n_used, bm):
    n, d = h.shape
    p_total = slot_tok.shape[0]
    rows = min(256, bm)
    steps = p_total // rows
    idx = slot_tok.reshape(steps, 1, rows)
    return pl.pallas_call(
        functools.partial(_gather_kernel, rows=rows, bm=bm, steps=steps),
        grid_spec=pltpu.PrefetchScalarGridSpec(
            num_scalar_prefetch=1, grid=(steps,),
            in_specs=[pl.BlockSpec((1, 1, rows), lambda i, nu: (i, 0, 0), memory_space=pltpu.SMEM),
                      pl.BlockSpec((1, 1, rows), lambda i, nu: (jnp.minimum(i + 1, steps - 1), 0, 0),
                                   memory_space=pltpu.SMEM),
                      pl.BlockSpec(memory_space=pl.ANY)],
            out_specs=pl.BlockSpec((rows, d), lambda i, nu: (i, 0)),
            scratch_shapes=[pltpu.VMEM((2, rows, d), F32), pltpu.SemaphoreType.DMA((2,))],
        ),
        out_shape=jax.ShapeDtypeStruct((p_total, d), BF16),
        compiler_params=_cparams(("arbitrary",)), name="moe_gather",
    )(n_used, idx, idx, h)


def _expert_changes(be_ref, b):
    return (b == 0) | (be_ref[b] != be_ref[jnp.maximum(b - 1, 0)])


W_PREP_ROWS = 256


def _gmm1_kernel(be_ref, nu_ref, x_ref, w_ref, bg_ref, bl_ref, o_ref, wg_scr, wl_scr):
    b = pl.program_id(1)
    used = b < nu_ref[0]
    kdim = w_ref.shape[2]
    tf = wg_scr.shape[1]

    @pl.when(used & _expert_changes(be_ref, b))
    def _():
        lane = lax.broadcasted_iota(I32, (W_PREP_ROWS, LANES), 1)
        even = (2 * lane) % LANES
        odd = even + 1
        low = lane < LANES // 2

        def rows(r, c):
            r0 = pl.multiple_of(r * W_PREP_ROWS, W_PREP_ROWS)
            for j in range(tf // LANES):
                a = w_ref[0, 0, pl.ds(r0, W_PREP_ROWS), 2 * j * LANES:(2 * j + 1) * LANES]
                z = w_ref[0, 0, pl.ds(r0, W_PREP_ROWS), (2 * j + 1) * LANES:(2 * j + 2) * LANES]
                g = jnp.where(low, jnp.take_along_axis(a, even, axis=1), jnp.take_along_axis(z, even, axis=1))
                u = jnp.where(low, jnp.take_along_axis(a, odd, axis=1), jnp.take_along_axis(z, odd, axis=1))
                wg_scr[pl.ds(r0, W_PREP_ROWS), j * LANES:(j + 1) * LANES] = g.astype(BF16)
                wl_scr[pl.ds(r0, W_PREP_ROWS), j * LANES:(j + 1) * LANES] = u.astype(BF16)
            return c

        lax.fori_loop(0, kdim // W_PREP_ROWS, rows, 0)

    @pl.when(used)
    def _():
        x = x_ref[...]
        g = jnp.dot(x, wg_scr[...], preferred_element_type=F32) + bg_ref[0]
        u = jnp.dot(x, wl_scr[...], preferred_element_type=F32) + bl_ref[0]
        g = jnp.minimum(g, SWIGLU_LIMIT)
        u = jnp.clip(u, -SWIGLU_LIMIT, SWIGLU_LIMIT)
        o_ref[...] = (g * jax.nn.sigmoid(SWIGLU_ALPHA * g) * (u + 1.0)).astype(BF16)

    @pl.when(jnp.logical_not(used))
    def _():
        o_ref[...] = jnp.zeros_like(o_ref)


def _gmm1_call(xs, w_gu, layer, bg, bl, be, n_used, bm):
    p_total, d = xs.shape
    _, n_exp, _, f2 = w_gu.shape
    f = f2 // 2
    tf = 512 if f % 512 == 0 else f
    nb = p_total // bm
    bspec = pl.BlockSpec((1, 1, tf), lambda n, b, be, nu: (be[b], 0, n))
    return pl.pallas_call(
        _gmm1_kernel,
        grid_spec=pltpu.PrefetchScalarGridSpec(
            num_scalar_prefetch=2, grid=(f // tf, nb),
            in_specs=[pl.BlockSpec((bm, d), lambda n, b, be, nu: (b, 0)),
                      pl.BlockSpec((1, 1, d, 2 * tf), lambda n, b, be, nu: (layer, be[b], 0, n)),
                      bspec, bspec],
            out_specs=pl.BlockSpec((bm, tf), lambda n, b, be, nu: (b, n)),
            scratch_shapes=[pltpu.VMEM((d, tf), BF16), pltpu.VMEM((d, tf), BF16)],
        ),
        out_shape=jax.ShapeDtypeStruct((p_total, f), BF16),
        compiler_params=_cparams(("arbitrary", "arbitrary")), name="moe_gate_up",
    )(be, n_used, xs, w_gu, bg, bl)


def _gmm2_kernel(be_ref, nu_ref, a_ref, w_ref, b_ref, o_ref, w_scr):
    b = pl.program_id(1)
    used = b < nu_ref[0]
    kdim = w_ref.shape[2]

    @pl.when(used & _expert_changes(be_ref, b))
    def _():
        def rows(r, c):
            r0 = pl.multiple_of(r * W_PREP_ROWS, W_PREP_ROWS)
            w_scr[pl.ds(r0, W_PREP_ROWS), :] = w_ref[0, 0, pl.ds(r0, W_PREP_ROWS), :].astype(BF16)
            return c

        lax.fori_loop(0, kdim // W_PREP_ROWS, rows, 0)

    @pl.when(used)
    def _():
        o_ref[...] = jnp.dot(a_ref[...], w_scr[...], preferred_element_type=F32) + b_ref[0]

    @pl.when(jnp.logical_not(used))
    def _():
        o_ref[...] = jnp.zeros_like(o_ref)


def _gmm2_call(act, wd, layer, bd, be, n_used, bm):
    p_total, f = act.shape
    _, n_exp, _, d = wd.shape
    tn = 1024 if d % 1024 == 0 else d
    nb = p_total // bm
    return pl.pallas_call(
        _gmm2_kernel,
        grid_spec=pltpu.PrefetchScalarGridSpec(
            num_scalar_prefetch=2, grid=(d // tn, nb),
            in_specs=[pl.BlockSpec((bm, f), lambda n, b, be, nu: (b, 0)),
                      pl.BlockSpec((1, 1, f, tn), lambda n, b, be, nu: (layer, be[b], 0, n)),
                      pl.BlockSpec((1, 1, tn), lambda n, b, be, nu: (be[b], 0, n))],
            out_specs=pl.BlockSpec((bm, tn), lambda n, b, be, nu: (b, n)),
            scratch_shapes=[pltpu.VMEM((f, tn), BF16)],
        ),
        out_shape=jax.ShapeDtypeStruct((p_total, d), F32),
        compiler_params=_cparams(("arbitrary", "arbitrary")), name="moe_down",
    )(be, n_used, act, wd, bd)


COMBINE_LANE_CHUNK = 512


def _combine_kernel(pos_cur, pos_nxt, gate_ref, g2_ref, x_ref, y_hbm, out_ref, buf, sem, *, tc, steps):
    i = pl.program_id(0)
    slot = i % 2
    n_slots = TOP_K * tc

    @pl.when(i == 0)
    def _():
        _row_gather(pos_cur, n_slots, y_hbm, buf.at[0], sem.at[0])

    @pl.when(i + 1 < steps)
    def _():
        _row_gather(pos_nxt, n_slots, y_hbm, buf.at[1 - slot], sem.at[1 - slot])

    _row_gather_wait(n_slots, y_hbm, buf.at[slot], sem.at[slot])
    gates = gate_ref[...]
    g2 = g2_ref[0]
    d = x_ref.shape[1]
    for c0 in range(0, d, COMBINE_LANE_CHUNK):
        seg = slice(c0, min(c0 + COMBINE_LANE_CHUNK, d))
        acc = None
        for k in range(TOP_K):
            term = gates[:, k:k + 1] * buf[slot, k * tc:(k + 1) * tc, seg]
            acc = term if acc is None else acc + term
        out_ref[:, seg] = x_ref[:, seg] + g2[:, seg] * acc


def _combine_call(x, y, pos, gates, mod48, *, n_prompt, dec_seq):
    n, d = x.shape
    tc = 128
    steps = n // tc
    row = functools.partial(_mod_row, tm=tc, n_prompt=n_prompt, dec_seq=dec_seq)
    idx = pos.reshape(TOP_K, steps, tc).transpose(1, 0, 2).reshape(steps, 1, TOP_K * tc)
    return pl.pallas_call(
        functools.partial(_combine_kernel, tc=tc, steps=steps),
        grid=(steps,),
        in_specs=[
            pl.BlockSpec((1, 1, TOP_K * tc), lambda i: (i, 0, 0), memory_space=pltpu.SMEM),
            pl.BlockSpec((1, 1, TOP_K * tc), lambda i: (jnp.minimum(i + 1, steps - 1), 0, 0),
                         memory_space=pltpu.SMEM),
            pl.BlockSpec((tc, TOP_K), lambda i: (i, 0)),
            pl.BlockSpec((1, 1, d), lambda i: (row(i) * N_MOD + 5, 0, 0)),
            pl.BlockSpec((tc, d), lambda i: (i, 0)),
            pl.BlockSpec(memory_space=pl.ANY),
        ],
        out_specs=pl.BlockSpec((tc, d), lambda i: (i, 0)),
        out_shape=jax.ShapeDtypeStruct(x.shape, F32),
        scratch_shapes=[pltpu.VMEM((2, TOP_K * tc, d), F32), pltpu.SemaphoreType.DMA((2,))],
        input_output_aliases={4: 0},
        compiler_params=_cparams(("arbitrary",)), name="moe_combine",
    )(idx, idx, gates, mod48, x, y)


def _final_kernel(x_ref, g_ref, o_ref):
    x = x_ref[...]
    o_ref[...] = (x * lax.rsqrt(jnp.mean(x * x, axis=-1, keepdims=True) + EPS)) * g_ref[...]


def _final_call(x, g, *, row_off, n_rows, tm):
    d = x.shape[1]
    mo = row_off // tm
    return pl.pallas_call(
        _final_kernel, grid=(n_rows // tm,),
        in_specs=[pl.BlockSpec((tm, d), lambda m: (m + mo, 0)), pl.BlockSpec((1, d), lambda m: (0, 0))],
        out_specs=pl.BlockSpec((tm, d), lambda m: (m, 0)),
        out_shape=jax.ShapeDtypeStruct((n_rows, d), F32),
        compiler_params=_cparams(("arbitrary",)), name="final_norm",
    )(x, g.reshape(1, d))


def _rope_tables(t):
    rows = t // GRID_W
    row = jnp.repeat(jnp.arange(rows), GRID_W).astype(F32)
    col = jnp.tile(jnp.arange(GRID_W), rows).astype(F32)
    nf = HEAD_DIM // 4
    inv = ROPE_THETA ** (-jnp.arange(nf, dtype=F32) / nf)
    ar = row[:, None] * inv
    ac = col[:, None] * inv
    ang = jnp.concatenate([ar, ar, ac, ac], axis=-1)
    sign = jnp.concatenate([-jnp.ones((nf,), F32), jnp.ones((nf,), F32)] * 2)
    return jnp.cos(ang), jnp.sin(ang) * sign


def _lambda_init(i):
    return 0.8 - 0.6 * math.exp(-0.3 * i)


def kernel(x_prompt, x_sample, cache_k_diff, cache_v_diff, cache_k_gqa, cache_v_gqa, c, c_ctx, w_ada, b_ada, norm_mix_g, norm_ffn_g, diff_w_qkv, diff_w_o, diff_lambda, diff_subln_g, gqa_w_qkv, gqa_w_o, gqa_q_norm_g, gqa_k_norm_g, moe_w_router, moe_b_router, moe_w_gate_up, moe_b_gate_up, moe_w_down, moe_b_down, final_norm_g):
    bp, tp, d = x_prompt.shape
    bs, ts, _ = x_sample.shape
    depth = w_ada.shape[0]
    n_exp = moe_w_router.shape[-1]
    d_ff = moe_w_down.shape[2]
    past = cache_k_diff.shape[2]
    n_prompt, n_sample = bp * tp, bs * ts
    n = n_prompt + n_sample
    tm = _tiles(n_prompt, ts)
    w_diff = d
    wq, wk = d, d // 2
    qscale = HEAD_DIM ** -0.5 * LOG2E

    x = jnp.concatenate([x_prompt.reshape(n_prompt, d), x_sample.reshape(n_sample, d)], axis=0)
    cond8 = jnp.zeros((8, d), F32).at[0].set(c_ctx).at[1:1 + bs].set(c)
    mod = _ada_call(cond8, w_ada, b_ada)
    rope = _rope_tables(ts)
    bm = _moe_block(n * TOP_K, n_exp)

    kd, vd, kg, vg = [], [], [], []
    for i in range(depth):
        mod48 = mod[i].reshape(8 * N_MOD, 1, d)
        h = _normmod_call(x, norm_mix_g[i], mod48, 1, 0, n_prompt, ts)
        j = i // 2
        if i % 2 == 0:
            w = diff_w_qkv[j].astype(BF16)
            pk = dict(row_off=0, n_rows=n_prompt, n_cols=w_diff, tm=tm)
            sk = dict(row_off=n_prompt, n_rows=n_sample, n_cols=w_diff, tm=tm)
            (q_p,) = _proj_call(h, w, col_off=0, scale=qscale, name="diff_q_p", **pk)
            k_p, k32 = _proj_call(h, w, col_off=w_diff, f32_out=True, name="diff_k_p", **pk)
            v_p, v32 = _proj_call(h, w, col_off=2 * w_diff, f32_out=True, name="diff_v_p", **pk)
            (q_s,) = _proj_call(h, w, col_off=0, rope=rope, scale=qscale, name="diff_q_s", **sk)
            (k_s,) = _proj_call(h, w, col_off=w_diff, rope=rope, name="diff_k_s", **sk)
            (v_s,) = _proj_call(h, w, col_off=2 * w_diff, name="diff_v_s", **sk)
            kd.append(k32.reshape(bp, tp, d // GROUP_LANES, GROUP_LANES))
            vd.append(v32.reshape(bp, tp, d // GROUP_LANES, GROUP_LANES))
            ctx = (cache_k_diff.reshape(bs, -1, past, w_diff), cache_v_diff.reshape(bs, -1, past, w_diff), j)
            ak = dict(diff=True, lam_p=diff_lambda[j], subln_g=diff_subln_g[j], lam_init=_lambda_init(i))
            kc, vc = w_diff, w_diff
            w_o = diff_w_o[j]
        else:
            w = gqa_w_qkv[j].astype(BF16)
            pk = dict(row_off=0, n_rows=n_prompt, tm=tm)
            sk = dict(row_off=n_prompt, n_rows=n_sample, tm=tm)
            qg, kgain = gqa_q_norm_g[j], gqa_k_norm_g[j]
            (q_p,) = _proj_call(h, w, col_off=0, n_cols=wq, gain=qg, scale=qscale, name="gqa_q_p", **pk)
            k_p, k32 = _proj_call(h, w, col_off=wq, n_cols=wk, gain=kgain, f32_out=True, name="gqa_k_p", **pk)
            v_p, v32 = _proj_call(h, w, col_off=wq + wk, n_cols=wk, f32_out=True, name="gqa_v_p", **pk)
            (q_s,) = _proj_call(h, w, col_off=0, n_cols=wq, gain=qg, rope=rope, scale=qscale, name="gqa_q_s", **sk)
            (k_s,) = _proj_call(h, w, col_off=wq, n_cols=wk, gain=kgain, rope=rope, name="gqa_k_s", **sk)
            (v_s,) = _proj_call(h, w, col_off=wq + wk, n_cols=wk, name="gqa_v_s", **sk)
            kg.append(k32.reshape(bp, tp, wk // HEAD_DIM, HEAD_DIM))
            vg.append(v32.reshape(bp, tp, wk // HEAD_DIM, HEAD_DIM))
            ctx = (cache_k_gqa.reshape(bs, -1, past, wk), cache_v_gqa.reshape(bs, -1, past, wk), j)
            ak = dict(diff=False)
            kc, vc = wk, wk
            w_o = gqa_w_o[j]
        o_p = _attn_call(q_p.reshape(bp, tp, d), k_p.reshape(bp, tp, kc), v_p.reshape(bp, tp, vc),
                         name="attn_p", **ak)
        o_s = _attn_call(q_s.reshape(bs, ts, d), k_s.reshape(bs, ts, kc), v_s.reshape(bs, ts, vc), ctx=ctx,
                         name="attn_s", **ak)
        w_o = w_o.astype(BF16)
        ok = dict(tm=tm, n_prompt=n_prompt, dec_seq=ts)
        x = _oproj_call(o_p.reshape(n_prompt, d), w_o, mod48, x, row_off=0, **ok)
        x = _oproj_call(o_s.reshape(n_sample, d), w_o, mod48, x, row_off=n_prompt, **ok)

        wr = jnp.zeros((d, LANES), F32).at[:, :n_exp].set(moe_w_router[i])
        wr_hi = wr.astype(BF16)
        wr_lo = (wr - wr_hi.astype(F32)).astype(BF16)
        br = jnp.zeros((1, LANES), F32).at[0, :n_exp].set(moe_b_router[i])
        h, topi, gates = _normmod_call(x, norm_ffn_g[i], mod48, 4, 3, n_prompt, ts,
                                       router=(wr_hi, wr_lo, br, n_exp))
        slot_tok, pos, be, n_used = _route_meta(topi, n_exp, bm)
        xs = _gather_call(h, slot_tok, n_used, bm)
        b_gu = moe_b_gate_up[i]
        act = _gmm1_call(xs, moe_w_gate_up, i, b_gu[:, None, 0::2], b_gu[:, None, 1::2], be, n_used, bm)
        yb = _gmm2_call(act, moe_w_down, i, moe_b_down[i][:, None, :], be, n_used, bm)
        x = _combine_call(x, yb, pos, gates.T, mod48, n_prompt=n_prompt, dec_seq=ts)

    y_p = _final_call(x, final_norm_g, row_off=0, n_rows=n_prompt, tm=tm)
    y_s = _final_call(x, final_norm_g, row_off=n_prompt, n_rows=n_sample, tm=tm)
    return (y_p.reshape(bp, tp, d), y_s.reshape(bs, ts, d),
            jnp.stack(kd, axis=1), jnp.stack(vd, axis=1), jnp.stack(kg, axis=1), jnp.stack(vg, axis=1))
```

```python
import functools
import math

import jax
import jax.numpy as jnp
from jax import lax
from jax.experimental import pallas as pl
from jax.experimental.pallas import tpu as pltpu

F32 = jnp.float32
BF16 = jnp.bfloat16
I32 = jnp.int32
U32 = jnp.uint32
BF16_HIGH_MASK = 0xFFFF0000

HEAD_DIM = 128
GROUP_LANES = 2 * HEAD_DIM
GRID_W = 64
ROPE_THETA = 10000.0
TOP_K = 4
SWIGLU_ALPHA = 1.702
SWIGLU_LIMIT = 7.0
EPS = 1e-6
N_MOD = 6
LOG2E = 1.4426950408889634

VMEM_LIMIT_BYTES = 56 * 1024 * 1024
LANES = 128
NEG_BIG = -1e30


def _cparams(sem):
    return pltpu.CompilerParams(dimension_semantics=sem, vmem_limit_bytes=VMEM_LIMIT_BYTES)


def _tiles(n_prompt, dec_seq, cap=512):
    tm = cap
    while n_prompt % tm or dec_seq % tm:
        tm //= 2
    return tm


def _mod_row(m, tm, n_prompt, dec_seq):
    mp = n_prompt // tm
    return jnp.where(m < mp, 0, 1 + (m - mp) // (dec_seq // tm))


def _ada_kernel(c_ref, w_ref, b_ref, o_ref):
    c = c_ref[...]
    a = (c * jax.nn.sigmoid(c)).astype(BF16)
    w = w_ref[0].astype(BF16)
    o_ref[0] = jnp.dot(a, w, preferred_element_type=F32) + b_ref[0]


def _ada_call(cond8, w_ada, b_ada):
    depth, d, nmod = w_ada.shape
    tn = 1024 if nmod % 1024 == 0 else nmod
    return pl.pallas_call(
        _ada_kernel,
        grid=(depth, nmod // tn),
        in_specs=[
            pl.BlockSpec((8, d), lambda l, n: (0, 0)),
            pl.BlockSpec((1, d, tn), lambda l, n: (l, 0, n)),
            pl.BlockSpec((1, 1, tn), lambda l, n: (l, 0, n)),
        ],
        out_specs=pl.BlockSpec((1, 8, tn), lambda l, n: (l, 0, n)),
        out_shape=jax.ShapeDtypeStruct((depth, 8, nmod), F32),
        compiler_params=_cparams(("arbitrary", "arbitrary")),
        name="ada_mod",
    )(cond8, w_ada, b_ada.reshape(depth, 1, nmod))


def _normmod(x, g, sc, sh):
    y = x * lax.rsqrt(jnp.mean(x * x, axis=-1, keepdims=True) + EPS)
    return (y * g) * (1.0 + sc) + sh


def _normmod_kernel(x_ref, g_ref, sc_ref, sh_ref, h_ref):
    h_ref[...] = _normmod(x_ref[...], g_ref[...], sc_ref[0], sh_ref[0]).astype(BF16)


def _normmod_router_kernel(x_ref, g_ref, sc_ref, sh_ref, wh_ref, wl_ref, br_ref,
                           h_ref, ti_ref, gt_ref, *, n_exp):
    h = _normmod(x_ref[...], g_ref[...], sc_ref[0], sh_ref[0])
    hi = h.astype(BF16)
    bits = pltpu.bitcast(hi.astype(F32), U32)
    half = bits.shape[1] // 2
    h_ref[...] = (bits[:, :half] >> 16) | (bits[:, half:] & jnp.uint32(BF16_HIGH_MASK))
    lo = (h - hi.astype(F32)).astype(BF16)
    wh = wh_ref[...]
    lg = (jnp.dot(hi, wh, preferred_element_type=F32)
          + jnp.dot(lo, wh, preferred_element_type=F32)
          + jnp.dot(hi, wl_ref[...], preferred_element_type=F32)) + br_ref[...]
    cur = lg.T[:n_exp]
    eid = lax.broadcasted_iota(I32, cur.shape, 0)
    vals, idxs = [], []
    for _ in range(TOP_K):
        mx = jnp.max(cur, axis=0, keepdims=True)
        ix = jnp.min(jnp.where(cur == mx, eid, n_exp), axis=0, keepdims=True)
        vals.append(mx)
        idxs.append(ix)
        cur = jnp.where(eid == ix, jnp.finfo(F32).min, cur)
    ex = [jnp.exp(v - vals[0]) for v in vals]
    den = ex[0] + ex[1] + ex[2] + ex[3]
    for k in range(TOP_K):
        ti_ref[k:k + 1, :] = idxs[k]
        gt_ref[k:k + 1, :] = ex[k] / den


def _normmod_call(x, g, mod48, chunk_sc, chunk_sh, n_prompt, dec_seq, router=None):
    n, d = x.shape
    tm = _tiles(n_prompt, dec_seq)
    row = functools.partial(_mod_row, tm=tm, n_prompt=n_prompt, dec_seq=dec_seq)
    in_specs = [
        pl.BlockSpec((tm, d), lambda m: (m, 0)),
        pl.BlockSpec((1, d), lambda m: (0, 0)),
        pl.BlockSpec((1, 1, d), lambda m: (row(m) * N_MOD + chunk_sc, 0, 0)),
        pl.BlockSpec((1, 1, d), lambda m: (row(m) * N_MOD + chunk_sh, 0, 0)),
    ]
    h_spec = pl.BlockSpec((tm, d), lambda m: (m, 0))
    h_shape = jax.ShapeDtypeStruct((n, d), BF16)
    if router is None:
        return pl.pallas_call(
            _normmod_kernel, grid=(n // tm,), in_specs=in_specs, out_specs=h_spec, out_shape=h_shape,
            compiler_params=_cparams(("arbitrary",)), name="normmod",
        )(x, g.reshape(1, d), mod48, mod48)
    wr_hi, wr_lo, br, n_exp = router
    in_specs += [
        pl.BlockSpec((d, LANES), lambda m: (0, 0)),
        pl.BlockSpec((d, LANES), lambda m: (0, 0)),
        pl.BlockSpec((1, LANES), lambda m: (0, 0)),
    ]
    kv_spec = pl.BlockSpec((TOP_K, tm), lambda m: (0, m))
    return pl.pallas_call(
        functools.partial(_normmod_router_kernel, n_exp=n_exp),
        grid=(n // tm,), in_specs=in_specs,
        out_specs=[pl.BlockSpec((tm, d // 2), lambda m: (m, 0)), kv_spec, kv_spec],
        out_shape=[jax.ShapeDtypeStruct((n, d // 2), U32), jax.ShapeDtypeStruct((TOP_K, n), I32),
                   jax.ShapeDtypeStruct((TOP_K, n), F32)],
        compiler_params=_cparams(("arbitrary",)), name="normmod_router",
    )(x, g.reshape(1, d), mod48, mod48, wr_hi, wr_lo, br)


def _proj_kernel(*refs, norm, rope, scale, f32_out, heads):
    it = iter(refs)
    x_ref, w_ref = next(it), next(it)
    gain_ref = next(it) if norm else None
    cos_ref, sin_ref = (next(it), next(it)) if rope else (None, None)
    o_ref = next(it)
    o32_ref = next(it) if f32_out else None
    x = x_ref[...]
    if rope:
        cos, sin = cos_ref[...], sin_ref[...]
        lane = lax.broadcasted_iota(I32, cos.shape, 1)
        first_half = (lane % (HEAD_DIM // 2)) < (HEAD_DIM // 4)
    for j in range(heads):
        seg = slice(j * HEAD_DIM, (j + 1) * HEAD_DIM)
        if j % 2 == 0:
            pair = slice(j * HEAD_DIM, min(j + 2, heads) * HEAD_DIM)
            acc = jnp.dot(x, w_ref[:, pair], preferred_element_type=F32)
        y = acc[:, (j % 2) * HEAD_DIM:(j % 2 + 1) * HEAD_DIM]
        if norm:
            y = (y * lax.rsqrt(jnp.mean(y * y, axis=-1, keepdims=True) + EPS)) * gain_ref[...]
        if f32_out:
            o32_ref[:, seg] = y
        if rope:
            r = jnp.where(first_half, pltpu.roll(y, HEAD_DIM - HEAD_DIM // 4, 1), pltpu.roll(y, HEAD_DIM // 4, 1))
            y = y * cos + r * sin
        if scale != 1.0:
            y = y * scale
        o_ref[:, seg] = y.astype(BF16)


def _proj_call(h, w, *, row_off, n_rows, col_off, n_cols, tm, gain=None, rope=None, scale=1.0,
               f32_out=False, name="proj"):
    d = h.shape[1]
    tn = 512 if n_cols % 512 == 0 else n_cols
    mo, no = row_off // tm, col_off // tn
    in_specs = [
        pl.BlockSpec((tm, d), lambda m, n: (m + mo, 0)),
        pl.BlockSpec((d, tn), lambda m, n: (0, n + no)),
    ]
    args = [h, w]
    if gain is not None:
        in_specs.append(pl.BlockSpec((1, HEAD_DIM), lambda m, n: (0, 0)))
        args.append(gain.reshape(1, HEAD_DIM))
    if rope is not None:
        cos, sin = rope
        tb = cos.shape[0] // tm
        in_specs += [pl.BlockSpec((tm, HEAD_DIM), lambda m, n: (m % tb, 0))] * 2
        args += [cos, sin]
    out_specs = [pl.BlockSpec((tm, tn), lambda m, n: (m, n))]
    out_shape = [jax.ShapeDtypeStruct((n_rows, n_cols), BF16)]
    if f32_out:
        out_specs.append(pl.BlockSpec((tm, tn), lambda m, n: (m, n)))
        out_shape.append(jax.ShapeDtypeStruct((n_rows, n_cols), F32))
    return pl.pallas_call(
        functools.partial(_proj_kernel, norm=gain is not None, rope=rope is not None, scale=scale,
                          f32_out=f32_out, heads=tn // HEAD_DIM),
        grid=(n_rows // tm, n_cols // tn), in_specs=in_specs, out_specs=out_specs, out_shape=out_shape,
        compiler_params=_cparams(("arbitrary", "arbitrary")), name=name,
    )(*args)


def _attn_kernel(*refs, diff, ctx, lam_init):
    it = iter(refs)
    lam_ref, subg_ref = (next(it), next(it)) if diff else (None, None)
    q_ref, k_ref, v_ref = next(it), next(it), next(it)
    ck_ref, cv_ref = (next(it), next(it)) if ctx else (None, None)
    o_ref = next(it)
    nt = (((1,), (1,)), ((), ()))
    ss = []
    for m in range(2):
        seg = slice(m * HEAD_DIM, (m + 1) * HEAD_DIM)
        q = q_ref[0, :, seg]
        s = [lax.dot_general(q, k_ref[0, :, seg] if diff else k_ref[0], nt, preferred_element_type=F32)]
        if ctx:
            ck = (ck_ref[0, 0, :, seg] if diff else ck_ref[0, 0]).astype(BF16)
            s.append(lax.dot_general(q, ck, nt, preferred_element_type=F32))
        ss.append(s)
    ps, ls = [], []
    for m in range(2):
        mx = functools.reduce(jnp.maximum, [jnp.max(s, axis=-1, keepdims=True) for s in ss[m]])
        p = [jnp.exp2(s - mx) for s in ss[m]]
        ls.append(functools.reduce(jnp.add, [jnp.sum(pp, axis=-1, keepdims=True) for pp in p]))
        ps.append([pp.astype(BF16) for pp in p])
    outs = []
    for m in range(2):
        acc = jnp.dot(ps[m][0], v_ref[0], preferred_element_type=F32)
        if ctx:
            acc = acc + jnp.dot(ps[m][1], cv_ref[0, 0].astype(BF16), preferred_element_type=F32)
        outs.append(acc / ls[m])
    if diff:
        lp = lam_ref[...]
        lam = (jnp.exp(jnp.sum(lp[0:1] * lp[1:2], axis=-1, keepdims=True))
               - jnp.exp(jnp.sum(lp[2:3] * lp[3:4], axis=-1, keepdims=True)) + lam_init)
        o = outs[0] - lam * outs[1]
        o = (o * lax.rsqrt(jnp.mean(o * o, axis=-1, keepdims=True) + EPS)) * subg_ref[...]
        o_ref[0] = (o * (1.0 - lam_init)).astype(BF16)
    else:
        o_ref[0, :, 0:HEAD_DIM] = outs[0].astype(BF16)
        o_ref[0, :, HEAD_DIM:2 * HEAD_DIM] = outs[1].astype(BF16)


def _attn_call(q, k, v, *, diff, ctx=None, lam_p=None, subln_g=None, lam_init=0.0, name="attn"):
    b, t, d = q.shape
    s = k.shape[1]
    groups = d // GROUP_LANES
    kw, vw = k.shape[2] // groups, v.shape[2] // groups
    tq = min(t, 256)
    assert t % tq == 0
    in_specs = [
        pl.BlockSpec((1, tq, GROUP_LANES), lambda bi, g, qi: (bi, qi, g)),
        pl.BlockSpec((1, s, kw), lambda bi, g, qi: (bi, 0, g)),
        pl.BlockSpec((1, s, vw), lambda bi, g, qi: (bi, 0, g)),
    ]
    args = [q, k, v]
    if ctx is not None:
        ck, cv, layer = ctx
        past = ck.shape[2]
        in_specs += [pl.BlockSpec((1, 1, past, kw), lambda bi, g, qi: (bi, layer, 0, g)),
                     pl.BlockSpec((1, 1, past, vw), lambda bi, g, qi: (bi, layer, 0, g))]
        args += [ck, cv]
    if diff:
        in_specs = [pl.BlockSpec((4, HEAD_DIM), lambda bi, g, qi: (0, 0)),
                    pl.BlockSpec((1, GROUP_LANES), lambda bi, g, qi: (0, 0))] + in_specs
        args = [lam_p, subln_g.reshape(1, GROUP_LANES)] + args
    return pl.pallas_call(
        functools.partial(_attn_kernel, diff=diff, ctx=ctx is not None, lam_init=lam_init),
        grid=(b, groups, t // tq), in_specs=in_specs,
        out_specs=pl.BlockSpec((1, tq, GROUP_LANES), lambda bi, g, qi: (bi, qi, g)),
        out_shape=jax.ShapeDtypeStruct((b, t, d), BF16),
        compiler_params=_cparams(("arbitrary", "arbitrary", "arbitrary")), name=name,
    )(*args)


def _oproj_kernel(o_ref, w_ref, g_ref, x_ref, out_ref):
    y = jnp.dot(o_ref[...], w_ref[...], preferred_element_type=F32)
    out_ref[...] = x_ref[...] + g_ref[0] * y


def _oproj_call(o, w, mod48, x, *, row_off, tm, n_prompt, dec_seq):
    n_rows, d_in = o.shape
    d = x.shape[1]
    tn = 512 if d % 512 == 0 else d
    mo = row_off // tm
    row = functools.partial(_mod_row, tm=tm, n_prompt=n_prompt, dec_seq=dec_seq)
    return pl.pallas_call(
        _oproj_kernel,
        grid=(n_rows // tm, d // tn),
        in_specs=[
            pl.BlockSpec((tm, d_in), lambda m, n: (m, 0)),
            pl.BlockSpec((d_in, tn), lambda m, n: (0, n)),
            pl.BlockSpec((1, 1, tn), lambda m, n: (row(m + mo) * N_MOD + 2, 0, n)),
            pl.BlockSpec((tm, tn), lambda m, n: (m + mo, n)),
        ],
        out_specs=pl.BlockSpec((tm, tn), lambda m, n: (m + mo, n)),
        out_shape=jax.ShapeDtypeStruct(x.shape, F32),
        input_output_aliases={3: 0},
        compiler_params=_cparams(("arbitrary", "arbitrary")), name="oproj_resid",
    )(o, w, mod48, x)


def _moe_block(n_slots, n_exp):
    bm = 512
    while bm > 128 and n_slots // n_exp < 2 * bm:
        bm //= 2
    return bm


def _route_meta(topi, n_exp, bm):
    k, n = topi.shape
    nk = n * k
    e_flat = topi.T.reshape(nk)
    onehot = (e_flat[:, None] == jnp.arange(n_exp, dtype=I32)[None, :]).astype(I32)
    csum = jnp.cumsum(onehot, axis=0)
    rank = jnp.take_along_axis(csum, e_flat[:, None], axis=1)[:, 0] - 1
    counts = csum[-1]
    padded = (counts + bm - 1) // bm * bm
    pends = jnp.cumsum(padded)
    pstarts = pends - padded
    pos = pstarts[e_flat] + rank
    p_total = -(-nk // bm) * bm + n_exp * bm
    slot_tok = jnp.zeros((p_total,), I32).at[pos].set(jnp.arange(nk, dtype=I32) // k)
    nb = p_total // bm
    n_used = pends[-1] // bm
    blk = jnp.arange(nb, dtype=I32)
    be = jnp.minimum(jnp.searchsorted(pends, blk * bm, side="right").astype(I32), n_exp - 1)
    be = jnp.where(blk < n_used, be, be[jnp.maximum(n_used - 1, 0)])
    return slot_tok, pos.reshape(n, k).T, be, n_used.reshape(1).astype(I32)


def _row_gather(idx_ref, n_rows, src_hbm, dst, sem):
    for r in range(n_rows):
        pltpu.make_async_copy(src_hbm.at[pl.ds(idx_ref[0, 0, r], 1)], dst.at[pl.ds(r, 1)], sem).start()


def _row_gather_wait(n_rows, src_hbm, dst, sem):
    pltpu.make_async_copy(src_hbm.at[pl.ds(0, n_rows)], dst, sem).wait()


def _gather_kernel(nu_ref, idx_cur, idx_nxt, h_hbm, o_ref, buf, sem, *, rows, bm, steps):
    i = pl.program_id(0)
    slot = i % 2
    rows_used = nu_ref[0] * bm

    @pl.when((i == 0) & (rows_used > 0))
    def _():
        _row_gather(idx_cur, rows, h_hbm, buf.at[0], sem.at[0])

    @pl.when((i + 1 < steps) & ((i + 1) * rows < rows_used))
    def _():
        _row_gather(idx_nxt, rows, h_hbm, buf.at[1 - slot], sem.at[1 - slot])

    @pl.when(i * rows < rows_used)
    def _():
        _row_gather_wait(rows, h_hbm, buf.at[slot], sem.at[slot])
        w = buf[slot]
        half = w.shape[1]
        o_ref[:, :half] = pltpu.bitcast(w << 16, F32).astype(BF16)
        o_ref[:, half:] = pltpu.bitcast(w & jnp.uint32(BF16_HIGH_MASK), F32).astype(BF16)

    @pl.when(i * rows >= rows_used)
    def _():
        o_ref[...] = jnp.zeros_like(o_ref)


def _gather_call(h, slot_tok, n_used, bm):
    n, half = h.shape
    d = 2 * half
    p_total = slot_tok.shape[0]
    rows = min(256, bm)
    steps = p_total // rows
    idx = slot_tok.reshape(steps, 1, rows)
    return pl.pallas_call(
        functools.partial(_gather_kernel, rows=rows, bm=bm, steps=steps),
        grid_spec=pltpu.PrefetchScalarGridSpec(
            num_scalar_prefetch=1, grid=(steps,),
            in_specs=[pl.BlockSpec((1, 1, rows), lambda i, nu: (i, 0, 0), memory_space=pltpu.SMEM),
                      pl.BlockSpec((1, 1, rows), lambda i, nu: (jnp.minimum(i + 1, steps - 1), 0, 0),
                                   memory_space=pltpu.SMEM),
                      pl.BlockSpec(memory_space=pl.ANY)],
            out_specs=pl.BlockSpec((rows, d), lambda i, nu: (i, 0)),
            scratch_shapes=[pltpu.VMEM((2, rows, half), U32), pltpu.SemaphoreType.DMA((2,))],
        ),
        out_shape=jax.ShapeDtypeStruct((p_total, d), BF16),
        compiler_params=_cparams(("arbitrary",)), name="moe_gather",
    )(n_used, idx, idx, h)


def _expert_changes(be_ref, b):
    return (b == 0) | (be_ref[b] != be_ref[jnp.maximum(b - 1, 0)])


W_PREP_ROWS = 256


def _gmm1_kernel(be_ref, nu_ref, x_ref, w_ref, bg_ref, bl_ref, o_ref, wg_scr, wl_scr):
    b = pl.program_id(1)
    used = b < nu_ref[0]
    kdim = w_ref.shape[2]
    tf = wg_scr.shape[1]

    @pl.when(used & _expert_changes(be_ref, b))
    def _():
        src = lax.broadcasted_iota(I32, (2 * LANES, 2 * LANES), 0)
        col = lax.broadcasted_iota(I32, (2 * LANES, 2 * LANES), 1)
        pick = jnp.where(col < LANES, 2 * col, 2 * (col - LANES) + 1)
        sel = jnp.where(src == pick, 1.0, 0.0).astype(BF16)

        def rows(r, c):
            r0 = pl.multiple_of(r * W_PREP_ROWS, W_PREP_ROWS)
            for j in range(tf // LANES):
                w = w_ref[0, 0, pl.ds(r0, W_PREP_ROWS), 2 * j * LANES:(2 * j + 2) * LANES].astype(BF16)
                gu = jnp.dot(w, sel, preferred_element_type=F32)
                wg_scr[pl.ds(r0, W_PREP_ROWS), j * LANES:(j + 1) * LANES] = gu[:, :LANES].astype(BF16)
                wl_scr[pl.ds(r0, W_PREP_ROWS), j * LANES:(j + 1) * LANES] = gu[:, LANES:].astype(BF16)
            return c

        lax.fori_loop(0, kdim // W_PREP_ROWS, rows, 0)

    @pl.when(used)
    def _():
        x = x_ref[...]
        g = jnp.dot(x, wg_scr[...], preferred_element_type=F32) + bg_ref[0]
        u = jnp.dot(x, wl_scr[...], preferred_element_type=F32) + bl_ref[0]
        g = jnp.minimum(g, SWIGLU_LIMIT)
        u = jnp.clip(u, -SWIGLU_LIMIT, SWIGLU_LIMIT)
        o_ref[...] = (g * jax.nn.sigmoid(SWIGLU_ALPHA * g) * (u + 1.0)).astype(BF16)

    @pl.when(jnp.logical_not(used))
    def _():
        o_ref[...] = jnp.zeros_like(o_ref)


def _gmm1_call(xs, w_gu, layer, bg, bl, be, n_used, bm):
    p_total, d = xs.shape
    _, n_exp, _, f2 = w_gu.shape
    f = f2 // 2
    tf = 512 if f % 512 == 0 else f
    nb = p_total // bm
    bspec = pl.BlockSpec((1, 1, tf), lambda n, b, be, nu: (be[b], 0, n))
    return pl.pallas_call(
        _gmm1_kernel,
        grid_spec=pltpu.PrefetchScalarGridSpec(
            num_scalar_prefetch=2, grid=(f // tf, nb),
            in_specs=[pl.BlockSpec((bm, d), lambda n, b, be, nu: (b, 0)),
                      pl.BlockSpec((1, 1, d, 2 * tf), lambda n, b, be, nu: (layer, be[b], 0, n)),
                      bspec, bspec],
            out_specs=pl.BlockSpec((bm, tf), lambda n, b, be, nu: (b, n)),
            scratch_shapes=[pltpu.VMEM((d, tf), BF16), pltpu.VMEM((d, tf), BF16)],
        ),
        out_shape=jax.ShapeDtypeStruct((p_total, f), BF16),
        compiler_params=_cparams(("arbitrary", "arbitrary")), name="moe_gate_up",
    )(be, n_used, xs, w_gu, bg, bl)


def _gmm2_kernel(be_ref, nu_ref, a_ref, w_ref, b_ref, o_ref, w_scr):
    b = pl.program_id(1)
    used = b < nu_ref[0]
    kdim = w_ref.shape[2]

    @pl.when(used & _expert_changes(be_ref, b))
    def _():
        def rows(r, c):
            r0 = pl.multiple_of(r * W_PREP_ROWS, W_PREP_ROWS)
            w_scr[pl.ds(r0, W_PREP_ROWS), :] = w_ref[0, 0, pl.ds(r0, W_PREP_ROWS), :].astype(BF16)
            return c

        lax.fori_loop(0, kdim // W_PREP_ROWS, rows, 0)

    @pl.when(used)
    def _():
        o_ref[...] = jnp.dot(a_ref[...], w_scr[...], preferred_element_type=F32) + b_ref[0]

    @pl.when(jnp.logical_not(used))
    def _():
        o_ref[...] = jnp.zeros_like(o_ref)


def _gmm2_call(act, wd, layer, bd, be, n_used, bm):
    p_total, f = act.shape
    _, n_exp, _, d = wd.shape
    tn = 1024 if d % 1024 == 0 else d
    nb = p_total // bm
    return pl.pallas_call(
        _gmm2_kernel,
        grid_spec=pltpu.PrefetchScalarGridSpec(
            num_scalar_prefetch=2, grid=(d // tn, nb),
            in_specs=[pl.BlockSpec((bm, f), lambda n, b, be, nu: (b, 0)),
                      pl.BlockSpec((1, 1, f, tn), lambda n, b, be, nu: (layer, be[b], 0, n)),
                      pl.BlockSpec((1, 1, tn), lambda n, b, be, nu: (be[b], 0, n))],
            out_specs=pl.BlockSpec((bm, tn), lambda n, b, be, nu: (b, n)),
            scratch_shapes=[pltpu.VMEM((f, tn), BF16)],
        ),
        out_shape=jax.ShapeDtypeStruct((p_total, d), F32),
        compiler_params=_cparams(("arbitrary", "arbitrary")), name="moe_down",
    )(be, n_used, act, wd, bd)


COMBINE_LANE_CHUNK = 512


def _combine_kernel(pos_cur, pos_nxt, gate_ref, g2_ref, x_ref, y_hbm, out_ref, buf, sem, *, tc, steps):
    i = pl.program_id(0)
    slot = i % 2
    n_slots = TOP_K * tc

    @pl.when(i == 0)
    def _():
        _row_gather(pos_cur, n_slots, y_hbm, buf.at[0], sem.at[0])

    @pl.when(i + 1 < steps)
    def _():
        _row_gather(pos_nxt, n_slots, y_hbm, buf.at[1 - slot], sem.at[1 - slot])

    _row_gather_wait(n_slots, y_hbm, buf.at[slot], sem.at[slot])
    gates = gate_ref[...]
    g2 = g2_ref[0]
    d = x_ref.shape[1]
    for c0 in range(0, d, COMBINE_LANE_CHUNK):
        seg = slice(c0, min(c0 + COMBINE_LANE_CHUNK, d))
        acc = None
        for k in range(TOP_K):
            term = gates[:, k:k + 1] * buf[slot, k * tc:(k + 1) * tc, seg]
            acc = term if acc is None else acc + term
        out_ref[:, seg] = x_ref[:, seg] + g2[:, seg] * acc


def _combine_call(x, y, pos, gates, mod48, *, n_prompt, dec_seq):
    n, d = x.shape
    tc = 128
    steps = n // tc
    row = functools.partial(_mod_row, tm=tc, n_prompt=n_prompt, dec_seq=dec_seq)
    idx = pos.reshape(TOP_K, steps, tc).transpose(1, 0, 2).reshape(steps, 1, TOP_K * tc)
    return pl.pallas_call(
        functools.partial(_combine_kernel, tc=tc, steps=steps),
        grid=(steps,),
        in_specs=[
            pl.BlockSpec((1, 1, TOP_K * tc), lambda i: (i, 0, 0), memory_space=pltpu.SMEM),
            pl.BlockSpec((1, 1, TOP_K * tc), lambda i: (jnp.minimum(i + 1, steps - 1), 0, 0),
                         memory_space=pltpu.SMEM),
            pl.BlockSpec((tc, TOP_K), lambda i: (i, 0)),
            pl.BlockSpec((1, 1, d), lambda i: (row(i) * N_MOD + 5, 0, 0)),
            pl.BlockSpec((tc, d), lambda i: (i, 0)),
            pl.BlockSpec(memory_space=pl.ANY),
        ],
        out_specs=pl.BlockSpec((tc, d), lambda i: (i, 0)),
        out_shape=jax.ShapeDtypeStruct(x.shape, F32),
        scratch_shapes=[pltpu.VMEM((2, TOP_K * tc, d), F32), pltpu.SemaphoreType.DMA((2,))],
        input_output_aliases={4: 0},
        compiler_params=_cparams(("arbitrary",)), name="moe_combine",
    )(idx, idx, gates, mod48, x, y)


def _final_kernel(x_ref, g_ref, o_ref):
    x = x_ref[...]
    o_ref[...] = (x * lax.rsqrt(jnp.mean(x * x, axis=-1, keepdims=True) + EPS)) * g_ref[...]


def _final_call(x, g, *, row_off, n_rows, tm):
    d = x.shape[1]
    mo = row_off // tm
    return pl.pallas_call(
        _final_kernel, grid=(n_rows // tm,),
        in_specs=[pl.BlockSpec((tm, d), lambda m: (m + mo, 0)), pl.BlockSpec((1, d), lambda m: (0, 0))],
        out_specs=pl.BlockSpec((tm, d), lambda m: (m, 0)),
        out_shape=jax.ShapeDtypeStruct((n_rows, d), F32),
        compiler_params=_cparams(("arbitrary",)), name="final_norm",
    )(x, g.reshape(1, d))


def _rope_tables(t):
    rows = t // GRID_W
    row = jnp.repeat(jnp.arange(rows), GRID_W).astype(F32)
    col = jnp.tile(jnp.arange(GRID_W), rows).astype(F32)
    nf = HEAD_DIM // 4
    inv = ROPE_THETA ** (-jnp.arange(nf, dtype=F32) / nf)
    ar = row[:, None] * inv
    ac = col[:, None] * inv
    ang = jnp.concatenate([ar, ar, ac, ac], axis=-1)
    sign = jnp.concatenate([-jnp.ones((nf,), F32), jnp.ones((nf,), F32)] * 2)
    return jnp.cos(ang), jnp.sin(ang) * sign


def _lambda_init(i):
    return 0.8 - 0.6 * math.exp(-0.3 * i)


def kernel(x_prompt, x_sample, cache_k_diff, cache_v_diff, cache_k_gqa, cache_v_gqa, c, c_ctx, w_ada, b_ada, norm_mix_g, norm_ffn_g, diff_w_qkv, diff_w_o, diff_lambda, diff_subln_g, gqa_w_qkv, gqa_w_o, gqa_q_norm_g, gqa_k_norm_g, moe_w_router, moe_b_router, moe_w_gate_up, moe_b_gate_up, moe_w_down, moe_b_down, final_norm_g):
    bp, tp, d = x_prompt.shape
    bs, ts, _ = x_sample.shape
    depth = w_ada.shape[0]
    n_exp = moe_w_router.shape[-1]
    d_ff = moe_w_down.shape[2]
    past = cache_k_diff.shape[2]
    n_prompt, n_sample = bp * tp, bs * ts
    n = n_prompt + n_sample
    tm = _tiles(n_prompt, ts, cap=1024)
    tm_norm = _tiles(n_prompt, ts)
    w_diff = d
    wq, wk = d, d // 2
    qscale = HEAD_DIM ** -0.5 * LOG2E

    x = jnp.concatenate([x_prompt.reshape(n_prompt, d), x_sample.reshape(n_sample, d)], axis=0)
    cond8 = jnp.zeros((8, d), F32).at[0].set(c_ctx).at[1:1 + bs].set(c)
    mod = _ada_call(cond8, w_ada, b_ada)
    rope = _rope_tables(ts)
    bm = _moe_block(n * TOP_K, n_exp)

    kd, vd, kg, vg = [], [], [], []
    for i in range(depth):
        mod48 = mod[i].reshape(8 * N_MOD, 1, d)
        h = _normmod_call(x, norm_mix_g[i], mod48, 1, 0, n_prompt, ts)
        j = i // 2
        if i % 2 == 0:
            w = diff_w_qkv[j].astype(BF16)
            pk = dict(row_off=0, n_rows=n_prompt, n_cols=w_diff, tm=tm)
            sk = dict(row_off=n_prompt, n_rows=n_sample, n_cols=w_diff, tm=tm)
            (q_p,) = _proj_call(h, w, col_off=0, scale=qscale, name="diff_q_p", **pk)
            k_p, k32 = _proj_call(h, w, col_off=w_diff, f32_out=True, name="diff_k_p", **pk)
            v_p, v32 = _proj_call(h, w, col_off=2 * w_diff, f32_out=True, name="diff_v_p", **pk)
            (q_s,) = _proj_call(h, w, col_off=0, rope=rope, scale=qscale, name="diff_q_s", **sk)
            (k_s,) = _proj_call(h, w, col_off=w_diff, rope=rope, name="diff_k_s", **sk)
            (v_s,) = _proj_call(h, w, col_off=2 * w_diff, name="diff_v_s", **sk)
            kd.append(k32.reshape(bp, tp, d // GROUP_LANES, GROUP_LANES))
            vd.append(v32.reshape(bp, tp, d // GROUP_LANES, GROUP_LANES))
            ctx = (cache_k_diff.reshape(bs, -1, past, w_diff), cache_v_diff.reshape(bs, -1, past, w_diff), j)
            ak = dict(diff=True, lam_p=diff_lambda[j], subln_g=diff_subln_g[j], lam_init=_lambda_init(i))
            kc, vc = w_diff, w_diff
            w_o = diff_w_o[j]
        else:
            w = gqa_w_qkv[j].astype(BF16)
            pk = dict(row_off=0, n_rows=n_prompt)
            sk = dict(row_off=n_prompt, n_rows=n_sample)
            qg, kgain = gqa_q_norm_g[j], gqa_k_norm_g[j]
            (q_p,) = _proj_call(h, w, col_off=0, n_cols=wq, gain=qg, scale=qscale, tm=tm_norm, name="gqa_q_p", **pk)
            k_p, k32 = _proj_call(h, w, col_off=wq, n_cols=wk, gain=kgain, f32_out=True, tm=tm_norm,
                                  name="gqa_k_p", **pk)
            v_p, v32 = _proj_call(h, w, col_off=wq + wk, n_cols=wk, f32_out=True, tm=tm, name="gqa_v_p", **pk)
            (q_s,) = _proj_call(h, w, col_off=0, n_cols=wq, gain=qg, rope=rope, scale=qscale, tm=tm_norm,
                                name="gqa_q_s", **sk)
            (k_s,) = _proj_call(h, w, col_off=wq, n_cols=wk, gain=kgain, rope=rope, tm=tm_norm, name="gqa_k_s", **sk)
            (v_s,) = _proj_call(h, w, col_off=wq + wk, n_cols=wk, tm=tm, name="gqa_v_s", **sk)
            kg.append(k32.reshape(bp, tp, wk // HEAD_DIM, HEAD_DIM))
            vg.append(v32.reshape(bp, tp, wk // HEAD_DIM, HEAD_DIM))
            ctx = (cache_k_gqa.reshape(bs, -1, past, wk), cache_v_gqa.reshape(bs, -1, past, wk), j)
            ak = dict(diff=False)
            kc, vc = wk, wk
            w_o = gqa_w_o[j]
        o_p = _attn_call(q_p.reshape(bp, tp, d), k_p.reshape(bp, tp, kc), v_p.reshape(bp, tp, vc),
                         name="attn_p", **ak)
        o_s = _attn_call(q_s.reshape(bs, ts, d), k_s.reshape(bs, ts, kc), v_s.reshape(bs, ts, vc), ctx=ctx,
                         name="attn_s", **ak)
        w_o = w_o.astype(BF16)
        ok = dict(tm=tm, n_prompt=n_prompt, dec_seq=ts)
        x = _oproj_call(o_p.reshape(n_prompt, d), w_o, mod48, x, row_off=0, **ok)
        x = _oproj_call(o_s.reshape(n_sample, d), w_o, mod48, x, row_off=n_prompt, **ok)

        wr = jnp.zeros((d, LANES), F32).at[:, :n_exp].set(moe_w_router[i])
        wr_hi = wr.astype(BF16)
        wr_lo = (wr - wr_hi.astype(F32)).astype(BF16)
        br = jnp.zeros((1, LANES), F32).at[0, :n_exp].set(moe_b_router[i])
        h, topi, gates = _normmod_call(x, norm_ffn_g[i], mod48, 4, 3, n_prompt, ts,
                                       router=(wr_hi, wr_lo, br, n_exp))
        slot_tok, pos, be, n_used = _route_meta(topi, n_exp, bm)
        xs = _gather_call(h, slot_tok, n_used, bm)
        b_gu = moe_b_gate_up[i]
        act = _gmm1_call(xs, moe_w_gate_up, i, b_gu[:, None, 0::2], b_gu[:, None, 1::2], be, n_used, bm)
        yb = _gmm2_call(act, moe_w_down, i, moe_b_down[i][:, None, :], be, n_used, bm)
        x = _combine_call(x, yb, pos, gates.T, mod48, n_prompt=n_prompt, dec_seq=ts)

    y_p = _final_call(x, final_norm_g, row_off=0, n_rows=n_prompt, tm=tm)
    y_s = _final_call(x, final_norm_g, row_off=n_prompt, n_rows=n_sample, tm=tm)
    return (y_p.reshape(bp, tp, d), y_s.reshape(bs, ts, d),
            jnp.stack(kd, axis=1), jnp.stack(vd, axis=1), jnp.stack(kg, axis=1), jnp.stack(vg, axis=1))
```
